```python
import math
import jax, jax.numpy as jnp
from jax import lax
import numpy as np

D_MODEL = 4096
BATCH = 4
SEQ = 4096
DEPTH = 1

D_FF = 11008
NORM_EPS = 1e-6
GDN_QK_HEADS = 16
GDN_V_HEADS = 32
GDN_HEAD_DIM = 128
GDN_QK_WIDTH = GDN_QK_HEADS * GDN_HEAD_DIM
GDN_V_WIDTH = GDN_V_HEADS * GDN_HEAD_DIM
GDN_CONV = 4
GDN_CHUNK = 64
MOBA_HEADS = 16
MOBA_HEAD_DIM = 128
MOBA_WIDTH = MOBA_HEADS * MOBA_HEAD_DIM
MOBA_BLOCK = 256
MOBA_TOP_K = 3
MOBA_Q_CHUNK = 16
ROPE_THETA = 10000.0
IN_SPLITS = (GDN_QK_WIDTH, GDN_QK_WIDTH, GDN_V_WIDTH, GDN_V_WIDTH, GDN_V_HEADS, GDN_V_HEADS,
             MOBA_WIDTH, MOBA_WIDTH, MOBA_WIDTH, D_MODEL, D_MODEL)
D_IN = sum(IN_SPLITS)

kernel_name = 'hybrid_gdn_moba_macaron_layer'


def rms_norm(x, gain):
    x32 = x.astype(jnp.float32)
    y = x32 * lax.rsqrt(jnp.mean(x32 * x32, axis=-1, keepdims=True) + NORM_EPS)
    return y.astype(x.dtype) * gain


def l2_norm(x):
    x32 = x.astype(jnp.float32)
    return x32 * lax.rsqrt(jnp.sum(x32 * x32, axis=-1, keepdims=True) + NORM_EPS)


def swiglu_ffn(h, w_gate, w_up, w_down):
    return (jax.nn.silu(h @ w_gate) * (h @ w_up)) @ w_down


def causal_depthwise_conv(x, w):
    K = w.shape[0]
    S = x.shape[1]
    xp = jnp.pad(x, ((0, 0), (K - 1, 0), (0, 0)))
    return sum(xp[:, j:j + S] * w[j] for j in range(K))


def rope(x, positions):
    half = x.shape[-1] // 2
    inv_freq = ROPE_THETA ** (-jnp.arange(half, dtype=jnp.float32) / half)
    ang = positions.astype(jnp.float32)[:, None] * inv_freq[None, :]
    cos = jnp.cos(ang)[None, :, None, :]
    sin = jnp.sin(ang)[None, :, None, :]
    x32 = x.astype(jnp.float32)
    x1, x2 = x32[..., :half], x32[..., half:]
    return jnp.concatenate([x1 * cos - x2 * sin, x2 * cos + x1 * sin], axis=-1).astype(x.dtype)


def gated_delta_rule_chunked(q, k, v, g, beta):
    B, H, S, Dk = q.shape
    Dv = v.shape[-1]
    C = GDN_CHUNK
    pad = (-S) % C
    padS = lambda t: jnp.pad(t.astype(jnp.float32), ((0, 0), (0, 0), (0, pad)) + ((0, 0),) * (t.ndim - 3))
    q, k, v, g, beta = padS(q), padS(k), padS(v), padS(g), padS(beta)
    NC = (S + pad) // C
    q = q * (Dk ** -0.5)
    q = q.reshape(B, H, NC, C, Dk)
    k = k.reshape(B, H, NC, C, Dk)
    v = v.reshape(B, H, NC, C, Dv)
    beta = beta.reshape(B, H, NC, C)
    g_cum = jnp.cumsum(g.reshape(B, H, NC, C), axis=-1)
    tril = jnp.tril(jnp.ones((C, C), dtype=bool))
    strict = jnp.tril(jnp.ones((C, C), dtype=bool), -1)
    diff = g_cum[..., :, None] - g_cum[..., None, :]
    decay = jnp.where(tril, jnp.exp(jnp.where(tril, diff, 0.0)), 0.0)
    k_beta = k * beta[..., None]
    v_beta = v * beta[..., None]
    L = jnp.where(strict, jnp.einsum('bhncd,bhnsd->bhncs', k_beta, k) * decay, 0.0)
    eye = jnp.eye(C, dtype=jnp.float32)
    T = lax.linalg.triangular_solve(eye + L, jnp.broadcast_to(eye, L.shape),
                                    left_side=True, lower=True, unit_diagonal=True)
    u = T @ v_beta
    w = T @ (k_beta * jnp.exp(g_cum)[..., None])
    attn_intra = jnp.where(tril, jnp.einsum('bhncd,bhnsd->bhncs', q, k) * decay, 0.0)

    def step(state, xs):
        q_c, k_c, u_c, w_c, a_c, g_c = xs
        v_new = u_c - w_c @ state
        o = (q_c * jnp.exp(g_c)[..., None]) @ state + a_c @ v_new
        g_last = g_c[..., -1]
        k_dec = k_c * jnp.exp(g_last[..., None] - g_c)[..., None]
        state = state * jnp.exp(g_last)[..., None, None] + jnp.einsum('bhcd,bhce->bhde', k_dec, v_new)
        return state, o

    front = lambda t: jnp.moveaxis(t, 2, 0)
    state0 = jnp.zeros((B, H, Dk, Dv), jnp.float32)
    _, o = lax.scan(step, state0, (front(q), front(k), front(u), front(w), front(attn_intra), front(g_cum)))
    o = jnp.moveaxis(o, 0, 2).reshape(B, H, NC * C, Dv)[:, :, :S]
    return o


def gated_deltanet(q, k, v, z, a, b, conv_w, A_log, dt_bias, o_norm):
    B, S, _ = q.shape
    qkv = jax.nn.silu(causal_depthwise_conv(jnp.concatenate([q, k, v], axis=-1), conv_w))
    q = qkv[..., :GDN_QK_WIDTH]
    k = qkv[..., GDN_QK_WIDTH:2 * GDN_QK_WIDTH]
    v = qkv[..., 2 * GDN_QK_WIDTH:]
    rep = GDN_V_HEADS // GDN_QK_HEADS
    q = jnp.repeat(l2_norm(q.reshape(B, S, GDN_QK_HEADS, GDN_HEAD_DIM)), rep, axis=2)
    k = jnp.repeat(l2_norm(k.reshape(B, S, GDN_QK_HEADS, GDN_HEAD_DIM)), rep, axis=2)
    v = v.reshape(B, S, GDN_V_HEADS, GDN_HEAD_DIM)
    g = -jnp.exp(A_log.astype(jnp.float32)) * jax.nn.softplus(a.astype(jnp.float32) + dt_bias.astype(jnp.float32))
    beta = jax.nn.sigmoid(b.astype(jnp.float32))
    o = gated_delta_rule_chunked(q.transpose(0, 2, 1, 3), k.transpose(0, 2, 1, 3), v.transpose(0, 2, 1, 3),
                                 g.transpose(0, 2, 1), beta.transpose(0, 2, 1))
    o = o.transpose(0, 2, 1, 3)
    o = rms_norm(o, o_norm) * jax.nn.silu(z.reshape(B, S, GDN_V_HEADS, GDN_HEAD_DIM).astype(jnp.float32))
    return o.reshape(B, S, GDN_V_WIDTH).astype(z.dtype)


def moba_attention(q, k, v, q_norm, k_norm):
    B, S, _ = q.shape
    H, D = MOBA_HEADS, MOBA_HEAD_DIM
    pos = jnp.arange(S)
    q = rope(rms_norm(q.reshape(B, S, H, D), q_norm), pos).transpose(0, 2, 1, 3)
    k = rope(rms_norm(k.reshape(B, S, H, D), k_norm), pos).transpose(0, 2, 1, 3)
    v = v.reshape(B, S, H, D).transpose(0, 2, 1, 3)
    nb = -(-S // MOBA_BLOCK)
    pad = nb * MOBA_BLOCK - S
    kb = jnp.pad(k, ((0, 0), (0, 0), (0, pad), (0, 0))).reshape(B, H, nb, MOBA_BLOCK, D)
    vb = jnp.pad(v, ((0, 0), (0, 0), (0, pad), (0, 0))).reshape(B, H, nb, MOBA_BLOCK, D)
    k_mean = jnp.mean(kb.astype(jnp.float32), axis=3)
    gate = jnp.einsum('bhsd,bhnd->bhsn', q.astype(jnp.float32), k_mean)
    q_blk = pos // MOBA_BLOCK
    past = jnp.arange(nb)[None, :] < q_blk[:, None]
    gate = jnp.where(past, gate, -jnp.inf)
    n_sel = min(MOBA_TOP_K, nb)
    _, top_idx = lax.top_k(gate, n_sel)
    sel_valid = jnp.arange(n_sel)[None, :] < q_blk[:, None]

    Qc = MOBA_Q_CHUNK
    n_chunks = S // Qc
    to_chunks = lambda t: jnp.moveaxis(t.reshape((B, H, n_chunks, Qc) + t.shape[3:]), 2, 0)
    q_c_all = to_chunks(q)
    idx_c_all = to_chunks(top_idx)
    valid_c_all = sel_valid.reshape(n_chunks, Qc, n_sel)
    b_ix = jnp.arange(B)[:, None, None, None]
    h_ix = jnp.arange(H)[None, :, None, None]
    scale = D ** -0.5

    def attend(args):
        c, q_c, idx, valid = args
        start = c * Qc
        own = start // MOBA_BLOCK
        k_own = lax.dynamic_slice_in_dim(kb, own, 1, axis=2)[:, :, 0]
        v_own = lax.dynamic_slice_in_dim(vb, own, 1, axis=2)[:, :, 0]
        k_sel = kb[b_ix, h_ix, idx]
        v_sel = vb[b_ix, h_ix, idx]
        s_sel = jnp.einsum('bhqd,bhqnkd->bhqnk', q_c, k_sel).astype(jnp.float32) * scale
        s_sel = jnp.where(valid[None, None, :, :, None], s_sel, -jnp.inf)
        s_own = jnp.einsum('bhqd,bhkd->bhqk', q_c, k_own).astype(jnp.float32) * scale
        q_pos = start + jnp.arange(Qc)
        k_pos = own * MOBA_BLOCK + jnp.arange(MOBA_BLOCK)
        s_own = jnp.where(k_pos[None, :] <= q_pos[:, None], s_own, -jnp.inf)
        s = jnp.concatenate([s_sel.reshape(B, H, Qc, n_sel * MOBA_BLOCK), s_own], axis=-1)
        p = jax.nn.softmax(s, axis=-1).astype(v.dtype)
        p_sel = p[..., :n_sel * MOBA_BLOCK].reshape(B, H, Qc, n_sel, MOBA_BLOCK)
        p_own = p[..., n_sel * MOBA_BLOCK:]
        return (jnp.einsum('bhqnk,bhqnkd->bhqd', p_sel, v_sel)
                + jnp.einsum('bhqk,bhkd->bhqd', p_own, v_own))

    o = lax.map(attend, (jnp.arange(n_chunks), q_c_all, idx_c_all, valid_c_all))
    o = jnp.moveaxis(o, 0, 2).reshape(B, H, S, D).transpose(0, 2, 1, 3)
    return o.reshape(B, S, MOBA_WIDTH)


def setup_inputs(seed: int = 0) -> dict:
    key = jax.random.key(seed)
    ks = jax.random.split(key, 20)
    f32 = jnp.float32
    L = DEPTH
    normal = lambda k, shape, fan_in: jax.random.normal(k, shape, f32) * (fan_in ** -0.5)
    gain = lambda k, shape: 1.0 + 0.02 * jax.random.normal(k, shape, f32)
    x = jax.random.normal(ks[0], (BATCH, SEQ, D_MODEL), f32)
    dt = jnp.exp(jax.random.uniform(ks[10], (L, GDN_V_HEADS), f32, math.log(1e-3), math.log(1e-1)))
    return {
        'x': x,
        'ffn1_norm': gain(ks[1], (L, D_MODEL)),
        'ffn1_w_gate': normal(ks[2], (L, D_MODEL, D_FF), D_MODEL),
        'ffn1_w_up': normal(ks[3], (L, D_MODEL, D_FF), D_MODEL),
        'ffn1_w_down': normal(ks[4], (L, D_FF, D_MODEL), D_FF),
        'mix_norm': gain(ks[5], (L, D_MODEL)),
        'w_in': normal(ks[6], (L, D_MODEL, D_IN), D_MODEL),
        'gdn_conv_w': normal(ks[7], (L, GDN_CONV, 2 * GDN_QK_WIDTH + GDN_V_WIDTH), GDN_CONV),
        'gdn_A_log': jnp.log(jax.random.uniform(ks[8], (L, GDN_V_HEADS), f32, 1.0, 16.0)),
        'gdn_dt_bias': dt + jnp.log(-jnp.expm1(-dt)),
        'gdn_o_norm': gain(ks[9], (L, GDN_HEAD_DIM)),
        'moba_q_norm': gain(ks[11], (L, MOBA_HEAD_DIM)),
        'moba_k_norm': gain(ks[12], (L, MOBA_HEAD_DIM)),
        'w_branch_gdn': normal(ks[13], (L, GDN_V_WIDTH, D_MODEL), GDN_V_WIDTH),
        'w_branch_moba': normal(ks[14], (L, MOBA_WIDTH, D_MODEL), MOBA_WIDTH),
        'w_out': normal(ks[15], (L, D_MODEL, D_MODEL), D_MODEL),
        'ffn2_norm': gain(ks[16], (L, D_MODEL)),
        'ffn2_w_gate': normal(ks[17], (L, D_MODEL, D_FF), D_MODEL),
        'ffn2_w_up': normal(ks[18], (L, D_MODEL, D_FF), D_MODEL),
        'ffn2_w_down': normal(ks[19], (L, D_FF, D_MODEL), D_FF),
    }


def reference(x, ffn1_norm, ffn1_w_gate, ffn1_w_up, ffn1_w_down, mix_norm, w_in, gdn_conv_w, gdn_A_log,
              gdn_dt_bias, gdn_o_norm, moba_q_norm, moba_k_norm, w_branch_gdn, w_branch_moba, w_out,
              ffn2_norm, ffn2_w_gate, ffn2_w_up, ffn2_w_down):
    offsets = np.cumsum(IN_SPLITS)[:-1].tolist()
    for l in range(DEPTH):
        h = rms_norm(x, ffn1_norm[l])
        x = x + 0.5 * swiglu_ffn(h, ffn1_w_gate[l], ffn1_w_up[l], ffn1_w_down[l])
        h = rms_norm(x, mix_norm[l])
        proj = h @ w_in[l]
        gq, gk, gv, gz, ga, gb, mq, mk, mv, gate_gdn, gate_moba = jnp.split(proj, offsets, axis=-1)
        o_gdn = gated_deltanet(gq, gk, gv, gz, ga, gb, gdn_conv_w[l], gdn_A_log[l], gdn_dt_bias[l], gdn_o_norm[l])
        o_moba = moba_attention(mq, mk, mv, moba_q_norm[l], moba_k_norm[l])
        y = (jax.nn.sigmoid(gate_gdn) * (o_gdn @ w_branch_gdn[l])
             + jax.nn.sigmoid(gate_moba) * (o_moba @ w_branch_moba[l]))
        x = x + y @ w_out[l]
        h = rms_norm(x, ffn2_norm[l])
        x = x + 0.5 * swiglu_ffn(h, ffn2_w_gate[l], ffn2_w_up[l], ffn2_w_down[l])
    return x
```

```python
import functools

import jax
import jax.numpy as jnp
from jax import lax
from jax.experimental import pallas as pl
from jax.experimental.pallas import tpu as pltpu

F32 = jnp.float32
BF16 = jnp.bfloat16

NORM_EPS = 1e-6
HEAD_DIM = 128
GDN_QK_HEADS = 16
GDN_V_HEADS = 32
GDN_CONV = 4
GDN_GROUP = 256
MOBA_HEADS = 16
MOBA_BLOCK = 256
MOBA_TOP_K = 3
ROPE_THETA = 10000.0
CONV_HALO = 8

VMEM_LIMIT = 56 * 1024 * 1024


def _params(*sem):
    return pltpu.CompilerParams(dimension_semantics=sem, vmem_limit_bytes=VMEM_LIMIT)


def _sigmoid(x):
    return 1.0 / (1.0 + jnp.exp(-x))


def _dot(a, b):
    return jnp.dot(a, b, preferred_element_type=F32)


def _dot_nt(a, b):
    return lax.dot_general(a, b, (((1,), (1,)), ((), ())), preferred_element_type=F32)


def _dot_tn(a, b):
    return lax.dot_general(a, b, (((0,), (0,)), ((), ())), preferred_element_type=F32)


def _rmsnorm_kernel(x_ref, g_ref, o_ref):
    x = x_ref[...]
    ms = jnp.mean(x * x, axis=-1, keepdims=True)
    o_ref[...] = (x * lax.rsqrt(ms + NORM_EPS) * g_ref[...]).astype(o_ref.dtype)


def rmsnorm_bf16(x, gain, tm=256):
    m, d = x.shape
    return pl.pallas_call(
        _rmsnorm_kernel,
        grid=(m // tm,),
        in_specs=[pl.BlockSpec((tm, d), lambda i: (i, 0)),
                  pl.BlockSpec((1, d), lambda i: (0, 0))],
        out_specs=pl.BlockSpec((tm, d), lambda i: (i, 0)),
        out_shape=jax.ShapeDtypeStruct((m, d), BF16),
        compiler_params=_params("parallel"),
    )(x, gain.reshape(1, d))


def _ffn_up_kernel(h_ref, wg_ref, wu_ref, o_ref):
    h = h_ref[...]
    g = _dot(h, wg_ref[...])
    u = _dot(h, wu_ref[...])
    o_ref[...] = (g * _sigmoid(g) * u).astype(o_ref.dtype)


def ffn_up(h, wg, wu, tm=1024, tn=256):
    m, k = h.shape
    n = wg.shape[1]
    return pl.pallas_call(
        _ffn_up_kernel,
        grid=(m // tm, n // tn),
        in_specs=[pl.BlockSpec((tm, k), lambda i, j: (i, 0)),
                  pl.BlockSpec((k, tn), lambda i, j: (0, j)),
                  pl.BlockSpec((k, tn), lambda i, j: (0, j))],
        out_specs=pl.BlockSpec((tm, tn), lambda i, j: (i, j)),
        out_shape=jax.ShapeDtypeStruct((m, n), BF16),
        compiler_params=_params("parallel", "arbitrary"),
    )(h, wg, wu)


def _mm_kernel(a_ref, b_ref, o_ref):
    o_ref[...] = _dot(a_ref[...], b_ref[...]).astype(o_ref.dtype)


def matmul(a, b, out_dtype, tm=1024, tn=512):
    m, k = a.shape
    n = b.shape[1]
    tn = min(tn, n)
    return pl.pallas_call(
        _mm_kernel,
        grid=(m // tm, n // tn),
        in_specs=[pl.BlockSpec((tm, k), lambda i, j: (i, 0)),
                  pl.BlockSpec((k, tn), lambda i, j: (0, j))],
        out_specs=pl.BlockSpec((tm, tn), lambda i, j: (i, j)),
        out_shape=jax.ShapeDtypeStruct((m, n), out_dtype),
        compiler_params=_params("parallel", "arbitrary"),
    )(a, b)


def _mm_residual_kernel(a_ref, b_ref, r_ref, o_ref, *, scale):
    o_ref[...] = r_ref[...] + scale * _dot(a_ref[...], b_ref[...])


def matmul_residual(a, b, res, scale, tm, tn):
    m, k = a.shape
    n = b.shape[1]
    return pl.pallas_call(
        functools.partial(_mm_residual_kernel, scale=scale),
        grid=(m // tm, n // tn),
        in_specs=[pl.BlockSpec((tm, k), lambda i, j: (i, 0)),
                  pl.BlockSpec((k, tn), lambda i, j: (0, j)),
                  pl.BlockSpec((tm, tn), lambda i, j: (i, j))],
        out_specs=pl.BlockSpec((tm, tn), lambda i, j: (i, j)),
        out_shape=jax.ShapeDtypeStruct((m, n), F32),
        compiler_params=_params("parallel", "arbitrary"),
    )(a, b, res)


def _merge_kernel(og_ref, om_ref, wg_ref, wm_ref, gg_ref, gm_ref, o_ref):
    yg = _dot(og_ref[...], wg_ref[...])
    ym = _dot(om_ref[...], wm_ref[...])
    o_ref[...] = (_sigmoid(gg_ref[...]) * yg + _sigmoid(gm_ref[...]) * ym).astype(o_ref.dtype)


def merge_branches(o_gdn, o_moba, wg, wm, proj, gg_off, gm_off, tm=1024, tn=256):
    m, kg = o_gdn.shape
    km = o_moba.shape[1]
    n = wg.shape[1]
    ggb, gmb = gg_off // tn, gm_off // tn
    return pl.pallas_call(
        _merge_kernel,
        grid=(m // tm, n // tn),
        in_specs=[pl.BlockSpec((tm, kg), lambda i, j: (i, 0)),
                  pl.BlockSpec((tm, km), lambda i, j: (i, 0)),
                  pl.BlockSpec((kg, tn), lambda i, j: (0, j)),
                  pl.BlockSpec((km, tn), lambda i, j: (0, j)),
                  pl.BlockSpec((tm, tn), lambda i, j: (i, ggb + j)),
                  pl.BlockSpec((tm, tn), lambda i, j: (i, gmb + j))],
        out_specs=pl.BlockSpec((tm, tn), lambda i, j: (i, j)),
        out_shape=jax.ShapeDtypeStruct((m, n), BF16),
        compiler_params=_params("parallel", "arbitrary"),
    )(o_gdn, o_moba, wg, wm, proj, proj)


def _gdn_prep_kernel(x_ref, halo_ref, w_ref, o_ref, *, tiles_per_seq, l2):
    i = pl.program_id(0)
    x = x_ref[...]
    ts = x.shape[0]
    halo = jnp.where(i % tiles_per_seq == 0, 0.0, halo_ref[...])
    xx = jnp.concatenate([halo, x], axis=0)
    acc = x * w_ref[GDN_CONV - 1:GDN_CONV, :]
    for j in range(GDN_CONV - 1):
        shifted = pltpu.roll(xx, GDN_CONV - 1 - j, axis=0)[CONV_HALO:CONV_HALO + ts]
        acc = acc + shifted * w_ref[j:j + 1, :]
    y = acc * _sigmoid(acc)
    if l2:
        for h in range(y.shape[1] // HEAD_DIM):
            yh = y[:, h * HEAD_DIM:(h + 1) * HEAD_DIM]
            ss = jnp.sum(yh * yh, axis=-1, keepdims=True)
            o_ref[:, h * HEAD_DIM:(h + 1) * HEAD_DIM] = yh * lax.rsqrt(ss + NORM_EPS)
    else:
        o_ref[...] = y


def gdn_prep(proj, conv_w, col_off, width, seq, l2, ts=512, tc=512):
    m = proj.shape[0]
    cb = col_off // tc
    hb = ts // CONV_HALO
    return pl.pallas_call(
        functools.partial(_gdn_prep_kernel, tiles_per_seq=seq // ts, l2=l2),
        grid=(m // ts, width // tc),
        in_specs=[pl.BlockSpec((ts, tc), lambda i, j: (i, cb + j)),
                  pl.BlockSpec((CONV_HALO, tc), lambda i, j: (jnp.maximum(i * hb - 1, 0), cb + j)),
                  pl.BlockSpec((GDN_CONV, tc), lambda i, j: (0, j))],
        out_specs=pl.BlockSpec((ts, tc), lambda i, j: (i, j)),
        out_shape=jax.ShapeDtypeStruct((m, width), F32),
        compiler_params=_params("parallel", "parallel"),
    )(proj, proj, conv_w)


def _split_bf16(a):
    hi = a.astype(BF16)
    lo = (a - hi.astype(F32)).astype(BF16)
    return hi, lo


def _mm3(a_split, b_split):
    ah, al = a_split
    bh, bl = b_split
    lhs = jnp.concatenate([ah, ah, al], axis=1)
    rhs = jnp.concatenate([bh, bl, bh], axis=0)
    return _dot(lhs, rhs)


def _unit_lower_inverse(lmat, eye, ij_xor):
    size = lmat.shape[0]
    dinv = eye - jnp.where(ij_xor == 1, lmat, 0.0)
    level = 1
    while (2 << level) <= size:
        coupling = jnp.where((ij_xor >> level) == 1, lmat, 0.0)
        ds = _split_bf16(dinv)
        dinv = dinv - _mm3(ds, _split_bf16(_mm3(_split_bf16(coupling), ds)))
        level += 1
    return dinv


def _gdn_kernel(q_ref, k_ref, v_ref, z_ref, a_ref, b_ref, alog_ref, dtb_ref, onorm_ref, o_ref, *, seq):
    c = GDN_GROUP
    d = HEAD_DIM
    heads = v_ref.shape[1] // d
    ii = lax.broadcasted_iota(jnp.int32, (c, c), 0)
    jj = lax.broadcasted_iota(jnp.int32, (c, c), 1)
    lower = ii >= jj
    strict = ii > jj
    eye_b = ii == jj
    eye = eye_b.astype(F32)
    ij_xor = ii ^ jj
    scale = d ** -0.5

    def group(gi, states):
        r0 = pl.multiple_of(gi * c, c)
        q = q_ref[pl.ds(r0, c), :] * scale
        k = k_ref[pl.ds(r0, c), :]
        kb16 = k.astype(BF16)
        kk = _dot_nt(kb16, kb16)
        qk = _dot_nt(q.astype(BF16), kb16)
        new_states = []
        for h in range(heads):
            s = states[h]
            a_row = a_ref[h:h + 1, pl.ds(r0, c)]
            b_row = b_ref[h:h + 1, pl.ds(r0, c)]
            x = a_row + dtb_ref[h:h + 1, :]
            softplus = jnp.maximum(x, 0.0) + jnp.log1p(jnp.exp(-jnp.abs(x)))
            g_row = -jnp.exp(alog_ref[h:h + 1, :]) * softplus
            beta_row = _sigmoid(b_row)
            gc_col = jnp.sum(jnp.where(lower, g_row, 0.0), axis=1, keepdims=True)
            gc_row = jnp.sum(jnp.where(eye_b, gc_col, 0.0), axis=0, keepdims=True)
            beta_col = jnp.sum(jnp.where(eye_b, beta_row, 0.0), axis=1, keepdims=True)
            decay = jnp.where(lower, jnp.exp(jnp.where(lower, gc_col - gc_row, 0.0)), 0.0)
            lmat = jnp.where(strict, beta_col * kk * decay, 0.0)
            t = _unit_lower_inverse(lmat, eye, ij_xor)
            v = v_ref[pl.ds(r0, c), h * d:(h + 1) * d]
            rhs = beta_col * jnp.concatenate([v, k * jnp.exp(gc_col)], axis=1)
            uw = _dot(t.astype(BF16), rhs.astype(BF16))
            u, w = uw[:, :d], uw[:, d:]
            s16 = s.astype(BF16)
            v_new = u - _dot(w.astype(BF16), s16)
            attn = jnp.where(lower, qk * decay, 0.0)
            o = _dot((q * jnp.exp(gc_col)).astype(BF16), s16) + _dot(attn.astype(BF16), v_new.astype(BF16))
            g_last = gc_col[c - 1:c, :]
            k_dec = k * jnp.exp(g_last - gc_col)
            new_states.append(s * jnp.exp(g_last) + _dot_tn(k_dec.astype(BF16), v_new.astype(BF16)))
            ms = jnp.mean(o * o, axis=-1, keepdims=True)
            z = z_ref[pl.ds(r0, c), h * d:(h + 1) * d]
            y = o * lax.rsqrt(ms + NORM_EPS) * onorm_ref[...] * (z * _sigmoid(z))
            o_ref[pl.ds(r0, c), h * d:(h + 1) * d] = y.astype(o_ref.dtype)
        return tuple(new_states)

    init = tuple(jnp.zeros((d, d), F32) for _ in range(heads))
    lax.fori_loop(0, seq // c, group, init)


def gdn_core(qk_act, v_act, proj, z_off, a_t, b_t, alog, dtb, o_norm, batch, seq):
    d = HEAD_DIM
    rep = GDN_V_HEADS // GDN_QK_HEADS
    m = qk_act.shape[0]
    zb = z_off // (rep * d)
    return pl.pallas_call(
        functools.partial(_gdn_kernel, seq=seq),
        grid=(batch, GDN_QK_HEADS),
        in_specs=[pl.BlockSpec((seq, d), lambda b, j: (b, j)),
                  pl.BlockSpec((seq, d), lambda b, j: (b, GDN_QK_HEADS + j)),
                  pl.BlockSpec((seq, rep * d), lambda b, j: (b, j)),
                  pl.BlockSpec((seq, rep * d), lambda b, j: (b, zb + j)),
                  pl.BlockSpec((None, None, rep, seq), lambda b, j: (b, j, 0, 0)),
                  pl.BlockSpec((None, None, rep, seq), lambda b, j: (b, j, 0, 0)),
                  pl.BlockSpec((None, rep, GDN_GROUP), lambda b, j: (j, 0, 0)),
                  pl.BlockSpec((None, rep, GDN_GROUP), lambda b, j: (j, 0, 0)),
                  pl.BlockSpec((1, d), lambda b, j: (0, 0))],
        out_specs=pl.BlockSpec((seq, rep * d), lambda b, j: (b, j)),
        out_shape=jax.ShapeDtypeStruct((m, GDN_V_HEADS * d), BF16),
        compiler_params=_params("parallel", "parallel"),
    )(qk_act, qk_act, v_act, proj, a_t, b_t, alog, dtb, o_norm.reshape(1, d))


def _moba_prep_kernel(x_ref, gain_ref, cos_ref, sin_ref, o_ref):
    d = HEAD_DIM
    for h in range(x_ref.shape[1] // d):
        x = x_ref[:, h * d:(h + 1) * d]
        ms = jnp.mean(x * x, axis=-1, keepdims=True)
        y = x * lax.rsqrt(ms + NORM_EPS) * gain_ref[...]
        o_ref[:, h * d:(h + 1) * d] = y * cos_ref[...] + pltpu.roll(y, d // 2, axis=1) * sin_ref[...]


def moba_prep(proj, col_off, width, gain, cos_t, sin_t, seq, ts=512, tc=512):
    m = proj.shape[0]
    d = HEAD_DIM
    cb = col_off // tc
    tps = seq // ts
    return pl.pallas_call(
        _moba_prep_kernel,
        grid=(m // ts, width // tc),
        in_specs=[pl.BlockSpec((ts, tc), lambda i, j: (i, cb + j)),
                  pl.BlockSpec((1, d), lambda i, j: (0, 0)),
                  pl.BlockSpec((ts, d), lambda i, j: (i % tps, 0)),
                  pl.BlockSpec((ts, d), lambda i, j: (i % tps, 0))],
        out_specs=pl.BlockSpec((ts, tc), lambda i, j: (i, j)),
        out_shape=jax.ShapeDtypeStruct((m, width), F32),
        compiler_params=_params("parallel", "parallel"),
    )(proj, gain.reshape(1, d), cos_t, sin_t)


def _moba_kernel(q_ref, k_ref, v_ref, o_ref, kmean_ref, *, nblk):
    blk = MOBA_BLOCK
    d = HEAD_DIM
    qb = pl.program_id(2)

    @pl.when(qb == 0)
    def _():
        for n in range(nblk):
            kmean_ref[n:n + 1, :] = jnp.mean(k_ref[n * blk:(n + 1) * blk, :], axis=0, keepdims=True)

    q = q_ref[...]
    gate = lax.dot_general(q, kmean_ref[...], (((1,), (1,)), ((), ())),
                           precision=lax.Precision.HIGHEST, preferred_element_type=F32)
    nidx = lax.broadcasted_iota(jnp.int32, (blk, nblk), 1)
    past = nidx < qb
    rank = jnp.zeros((blk, nblk), jnp.int32)
    for m in range(nblk):
        gm = gate[:, m:m + 1]
        beats = (gm > gate) | ((gm == gate) & (m < nidx))
        rank = rank + jnp.where(beats & (m < qb), 1, 0)
    sel = jnp.where(past & (rank < MOBA_TOP_K), 1.0, 0.0)

    q16 = (q * (d ** -0.5)).astype(BF16)
    r0 = pl.multiple_of(qb * blk, blk)
    s = _dot_nt(q16, k_ref[pl.ds(r0, blk), :].astype(BF16))
    ii = lax.broadcasted_iota(jnp.int32, (blk, blk), 0)
    jj = lax.broadcasted_iota(jnp.int32, (blk, blk), 1)
    s = jnp.where(jj <= ii, s, -jnp.inf)
    m0 = jnp.max(s, axis=-1, keepdims=True)
    p = jnp.exp(s - m0)
    l0 = jnp.sum(p, axis=-1, keepdims=True)
    acc0 = _dot(p.astype(BF16), v_ref[pl.ds(r0, blk), :].astype(BF16))

    def body(kb, carry):
        m_i, l_i, acc = carry
        c0 = pl.multiple_of(kb * blk, blk)
        s = _dot_nt(q16, k_ref[pl.ds(c0, blk), :].astype(BF16))
        chosen = jnp.sum(jnp.where(nidx == kb, sel, 0.0), axis=1, keepdims=True) > 0.0
        s = jnp.where(chosen, s, -jnp.inf)
        m_new = jnp.maximum(m_i, jnp.max(s, axis=-1, keepdims=True))
        alpha = jnp.exp(m_i - m_new)
        p = jnp.exp(s - m_new)
        l_new = alpha * l_i + jnp.sum(p, axis=-1, keepdims=True)
        acc = alpha * acc + _dot(p.astype(BF16), v_ref[pl.ds(c0, blk), :].astype(BF16))
        return m_new, l_new, acc

    _, l_f, acc_f = lax.fori_loop(0, qb, body, (m0, l0, acc0))
    o_ref[...] = (acc_f / l_f).astype(o_ref.dtype)


def moba_core(q_r, k_r, proj, v_off, batch, seq):
    d = HEAD_DIM
    m = q_r.shape[0]
    nblk = seq // MOBA_BLOCK
    vb = v_off // d
    return pl.pallas_call(
        functools.partial(_moba_kernel, nblk=nblk),
        grid=(batch, MOBA_HEADS, nblk),
        in_specs=[pl.BlockSpec((MOBA_BLOCK, d), lambda b, h, i: (b * nblk + i, h)),
                  pl.BlockSpec((seq, d), lambda b, h, i: (b, h)),
                  pl.BlockSpec((seq, d), lambda b, h, i: (b, vb + h))],
        out_specs=pl.BlockSpec((MOBA_BLOCK, d), lambda b, h, i: (b * nblk + i, h)),
        out_shape=jax.ShapeDtypeStruct((m, MOBA_HEADS * d), BF16),
        scratch_shapes=[pltpu.VMEM((nblk, d), F32)],
        compiler_params=_params("parallel", "parallel", "arbitrary"),
    )(q_r, k_r, proj)


def _swiglu_half_step(x, norm, w_gate, w_up, w_down):
    h = rmsnorm_bf16(x, norm)
    act = ffn_up(h, w_gate.astype(BF16), w_up.astype(BF16))
    return matmul_residual(act, w_down.astype(BF16), x, 0.5, tm=512, tn=256)


def _rope_tables(seq):
    half = HEAD_DIM // 2
    inv_freq = ROPE_THETA ** (-jnp.arange(half, dtype=F32) / half)
    ang = jnp.arange(seq).astype(F32)[:, None] * inv_freq[None, :]
    cos, sin = jnp.cos(ang), jnp.sin(ang)
    return jnp.concatenate([cos, cos], axis=1), jnp.concatenate([-sin, sin], axis=1)


def _layer(x, ffn1_norm, ffn1_w_gate, ffn1_w_up, ffn1_w_down, mix_norm, w_in, gdn_conv_w, gdn_A_log,
           gdn_dt_bias, gdn_o_norm, moba_q_norm, moba_k_norm, w_branch_gdn, w_branch_moba, w_out,
           ffn2_norm, ffn2_w_gate, ffn2_w_up, ffn2_w_down, batch, seq):
    d = HEAD_DIM
    qkw = GDN_QK_HEADS * d
    vw = GDN_V_HEADS * d
    mw = MOBA_HEADS * d
    dm = x.shape[1]

    x = _swiglu_half_step(x, ffn1_norm, ffn1_w_gate, ffn1_w_up, ffn1_w_down)

    ab_off = 2 * qkw + 2 * vw
    w_main = jnp.concatenate([w_in[:, :ab_off], w_in[:, ab_off + 2 * GDN_V_HEADS:]], axis=1).astype(BF16)
    w_ab = jnp.pad(w_in[:, ab_off:ab_off + 2 * GDN_V_HEADS], ((0, 0), (0, d - 2 * GDN_V_HEADS))).astype(BF16)
    q_off, v_off, z_off = 0, 2 * qkw, 2 * qkw + vw
    mq_off = z_off + vw
    mk_off, mv_off = mq_off + mw, mq_off + 2 * mw
    gg_off = mq_off + 3 * mw
    gm_off = gg_off + dm

    h = rmsnorm_bf16(x, mix_norm)
    proj = matmul(h, w_main, F32)
    ab = matmul(h, w_ab, F32)

    rep = GDN_V_HEADS // GDN_QK_HEADS
    qk_act = gdn_prep(proj, gdn_conv_w[:, :2 * qkw], q_off, 2 * qkw, seq, l2=True)
    v_act = gdn_prep(proj, gdn_conv_w[:, 2 * qkw:], v_off, vw, seq, l2=False)
    to_heads = lambda t: t.reshape(batch, seq, GDN_QK_HEADS, rep).transpose(0, 2, 3, 1)
    a_t = to_heads(ab[:, :GDN_V_HEADS])
    b_t = to_heads(ab[:, GDN_V_HEADS:2 * GDN_V_HEADS])
    per_head = lambda t: jnp.broadcast_to(t.astype(F32).reshape(GDN_QK_HEADS, rep, 1), (GDN_QK_HEADS, rep, GDN_GROUP))
    o_gdn = gdn_core(qk_act, v_act, proj, z_off, a_t, b_t, per_head(gdn_A_log), per_head(gdn_dt_bias),
                     gdn_o_norm, batch, seq)

    cos_t, sin_t = _rope_tables(seq)
    q_r = moba_prep(proj, mq_off, mw, moba_q_norm, cos_t, sin_t, seq)
    k_r = moba_prep(proj, mk_off, mw, moba_k_norm, cos_t, sin_t, seq)
    o_moba = moba_core(q_r, k_r, proj, mv_off, batch, seq)

    y = merge_branches(o_gdn, o_moba, w_branch_gdn.astype(BF16), w_branch_moba.astype(BF16), proj, gg_off, gm_off)
    x = matmul_residual(y, w_out.astype(BF16), x, 1.0, tm=1024, tn=256)

    return _swiglu_half_step(x, ffn2_norm, ffn2_w_gate, ffn2_w_up, ffn2_w_down)


def kernel(x, ffn1_norm, ffn1_w_gate, ffn1_w_up, ffn1_w_down, mix_norm, w_in, gdn_conv_w, gdn_A_log, gdn_dt_bias,
           gdn_o_norm, moba_q_norm, moba_k_norm, w_branch_gdn, w_branch_moba, w_out, ffn2_norm, ffn2_w_gate,
           ffn2_w_up, ffn2_w_down):
    batch, seq, dm = x.shape
    assert ffn1_norm.shape[0] == 1, "single-layer stack"
    out = _layer(x.reshape(batch * seq, dm), ffn1_norm[0], ffn1_w_gate[0], ffn1_w_up[0], ffn1_w_down[0],
                 mix_norm[0], w_in[0], gdn_conv_w[0], gdn_A_log[0], gdn_dt_bias[0], gdn_o_norm[0],
                 moba_q_norm[0], moba_k_norm[0], w_branch_gdn[0], w_branch_moba[0], w_out[0],
                 ffn2_norm[0], ffn2_w_gate[0], ffn2_w_up[0], ffn2_w_down[0], batch, seq)
    return out.reshape(batch, seq, dm)
```

```python
import functools

import jax
import jax.numpy as jnp
from jax import lax
from jax.experimental import pallas as pl
from jax.experimental.pallas import tpu as pltpu

F32 = jnp.float32
BF16 = jnp.bfloat16

NORM_EPS = 1e-6
HEAD_DIM = 128
GDN_QK_HEADS = 16
GDN_V_HEADS = 32
GDN_CONV = 4
GDN_GROUP = 256
GDN_GROUPS_PER_STEP = 4
MOBA_HEADS = 16
MOBA_HEADS_PER_STEP = 2
MOBA_BLOCK = 256
MOBA_TOP_K = 3
MOBA_CHUNK = 4
ROPE_THETA = 10000.0
CONV_HALO = 8
SUBLANES = 8

VMEM_LIMIT = 56 * 1024 * 1024


def _params(*sem):
    return pltpu.CompilerParams(dimension_semantics=sem, vmem_limit_bytes=VMEM_LIMIT)


def _sigmoid(x):
    return 1.0 / (1.0 + jnp.exp(-x))


def _dot(a, b):
    return jnp.dot(a, b, preferred_element_type=F32)


def _dot_nt(a, b):
    return lax.dot_general(a, b, (((1,), (1,)), ((), ())), preferred_element_type=F32)


def _dot_tn(a, b):
    return lax.dot_general(a, b, (((0,), (0,)), ((), ())), preferred_element_type=F32)


def _rmsnorm_kernel(x_ref, g_ref, o_ref):
    x = x_ref[...]
    ms = jnp.mean(x * x, axis=-1, keepdims=True)
    o_ref[...] = (x * lax.rsqrt(ms + NORM_EPS) * g_ref[...]).astype(o_ref.dtype)


def rmsnorm_bf16(x, gain, tm=256):
    m, d = x.shape
    return pl.pallas_call(
        _rmsnorm_kernel,
        grid=(m // tm,),
        in_specs=[pl.BlockSpec((tm, d), lambda i: (i, 0)),
                  pl.BlockSpec((1, d), lambda i: (0, 0))],
        out_specs=pl.BlockSpec((tm, d), lambda i: (i, 0)),
        out_shape=jax.ShapeDtypeStruct((m, d), BF16),
        compiler_params=_params("parallel"),
        name="rmsnorm",
    )(x, gain.reshape(1, d))


def _ffn_up_kernel(h_ref, wg_ref, wu_ref, o_ref):
    h = h_ref[...]
    g = _dot(h, wg_ref[...])
    u = _dot(h, wu_ref[...])
    o_ref[...] = (g * _sigmoid(g) * u).astype(o_ref.dtype)


def ffn_up(h, wg, wu, tm=1024, tn=256):
    m, k = h.shape
    n = wg.shape[1]
    return pl.pallas_call(
        _ffn_up_kernel,
        grid=(m // tm, n // tn),
        in_specs=[pl.BlockSpec((tm, k), lambda i, j: (i, 0)),
                  pl.BlockSpec((k, tn), lambda i, j: (0, j)),
                  pl.BlockSpec((k, tn), lambda i, j: (0, j))],
        out_specs=pl.BlockSpec((tm, tn), lambda i, j: (i, j)),
        out_shape=jax.ShapeDtypeStruct((m, n), BF16),
        compiler_params=_params("parallel", "arbitrary"),
        name="ffn_up",
    )(h, wg, wu)


def _mm_kernel(a_ref, b_ref, o_ref):
    o_ref[...] = _dot(a_ref[...], b_ref[...]).astype(o_ref.dtype)


def matmul(a, b, out_dtype, tm=1024, tn=512):
    m, k = a.shape
    n = b.shape[1]
    tn = min(tn, n)
    return pl.pallas_call(
        _mm_kernel,
        grid=(m // tm, n // tn),
        in_specs=[pl.BlockSpec((tm, k), lambda i, j: (i, 0)),
                  pl.BlockSpec((k, tn), lambda i, j: (0, j))],
        out_specs=pl.BlockSpec((tm, tn), lambda i, j: (i, j)),
        out_shape=jax.ShapeDtypeStruct((m, n), out_dtype),
        compiler_params=_params("parallel", "arbitrary"),
        name="matmul",
    )(a, b)


def _mm_residual_kernel(a_ref, b_ref, r_ref, o_ref, *, scale):
    o_ref[...] = r_ref[...] + scale * _dot(a_ref[...], b_ref[...])


def matmul_residual(a, b, res, scale, tm, tn):
    m, k = a.shape
    n = b.shape[1]
    return pl.pallas_call(
        functools.partial(_mm_residual_kernel, scale=scale),
        grid=(m // tm, n // tn),
        in_specs=[pl.BlockSpec((tm, k), lambda i, j: (i, 0)),
                  pl.BlockSpec((k, tn), lambda i, j: (0, j)),
                  pl.BlockSpec((tm, tn), lambda i, j: (i, j))],
        out_specs=pl.BlockSpec((tm, tn), lambda i, j: (i, j)),
        out_shape=jax.ShapeDtypeStruct((m, n), F32),
        compiler_params=_params("parallel", "arbitrary"),
        name="matmul_residual",
    )(a, b, res)


def _merge_kernel(og_ref, om_ref, wg_ref, wm_ref, gg_ref, gm_ref, o_ref):
    yg = _dot(og_ref[...], wg_ref[...])
    ym = _dot(om_ref[...], wm_ref[...])
    o_ref[...] = (_sigmoid(gg_ref[...]) * yg + _sigmoid(gm_ref[...]) * ym).astype(o_ref.dtype)


def merge_branches(o_gdn, o_moba, wg, wm, proj, gg_off, gm_off, tm=1024, tn=256):
    m, kg = o_gdn.shape
    km = o_moba.shape[1]
    n = wg.shape[1]
    ggb, gmb = gg_off // tn, gm_off // tn
    return pl.pallas_call(
        _merge_kernel,
        grid=(m // tm, n // tn),
        in_specs=[pl.BlockSpec((tm, kg), lambda i, j: (i, 0)),
                  pl.BlockSpec((tm, km), lambda i, j: (i, 0)),
                  pl.BlockSpec((kg, tn), lambda i, j: (0, j)),
                  pl.BlockSpec((km, tn), lambda i, j: (0, j)),
                  pl.BlockSpec((tm, tn), lambda i, j: (i, ggb + j)),
                  pl.BlockSpec((tm, tn), lambda i, j: (i, gmb + j))],
        out_specs=pl.BlockSpec((tm, tn), lambda i, j: (i, j)),
        out_shape=jax.ShapeDtypeStruct((m, n), BF16),
        compiler_params=_params("parallel", "arbitrary"),
        name="merge_branches",
    )(o_gdn, o_moba, wg, wm, proj, proj)


def _gdn_prep_kernel(x_ref, halo_ref, w_ref, o_ref, *, tiles_per_seq, l2):
    i = pl.program_id(0)
    x = x_ref[...]
    ts = x.shape[0]
    halo = jnp.where(i % tiles_per_seq == 0, 0.0, halo_ref[...])
    xx = jnp.concatenate([halo, x], axis=0)
    acc = x * w_ref[GDN_CONV - 1:GDN_CONV, :]
    for j in range(GDN_CONV - 1):
        shifted = pltpu.roll(xx, GDN_CONV - 1 - j, axis=0)[CONV_HALO:CONV_HALO + ts]
        acc = acc + shifted * w_ref[j:j + 1, :]
    y = acc * _sigmoid(acc)
    if l2:
        for h in range(y.shape[1] // HEAD_DIM):
            yh = y[:, h * HEAD_DIM:(h + 1) * HEAD_DIM]
            ss = jnp.sum(yh * yh, axis=-1, keepdims=True)
            o_ref[:, h * HEAD_DIM:(h + 1) * HEAD_DIM] = yh * lax.rsqrt(ss + NORM_EPS)
    else:
        o_ref[...] = y


def gdn_prep(proj, conv_w, col_off, width, seq, l2, ts=512, tc=512):
    m = proj.shape[0]
    cb = col_off // tc
    hb = ts // CONV_HALO
    return pl.pallas_call(
        functools.partial(_gdn_prep_kernel, tiles_per_seq=seq // ts, l2=l2),
        grid=(m // ts, width // tc),
        in_specs=[pl.BlockSpec((ts, tc), lambda i, j: (i, cb + j)),
                  pl.BlockSpec((CONV_HALO, tc), lambda i, j: (jnp.maximum(i * hb - 1, 0), cb + j)),
                  pl.BlockSpec((GDN_CONV, tc), lambda i, j: (0, j))],
        out_specs=pl.BlockSpec((ts, tc), lambda i, j: (i, j)),
        out_shape=jax.ShapeDtypeStruct((m, width), F32),
        compiler_params=_params("parallel", "parallel"),
        name="gdn_prep",
    )(proj, proj, conv_w)


def _split_bf16(a):
    hi = a.astype(BF16)
    lo = (a - hi.astype(F32)).astype(BF16)
    return hi, lo


def _mm3(a_split, b_split):
    ah, al = a_split
    bh, bl = b_split
    lhs = jnp.concatenate([ah, ah, al], axis=1)
    rhs = jnp.concatenate([bh, bl, bh], axis=0)
    return _dot(lhs, rhs)


def _unit_lower_inverses(lmats, eye, ij_xor):
    size = eye.shape[0]
    dinvs = [eye - jnp.where(ij_xor == 1, lm, 0.0) for lm in lmats]
    level = 1
    while (2 << level) <= size:
        mask = (ij_xor >> level) == 1
        d16 = [dv.astype(BF16) for dv in dinvs]
        cd = [_dot(jnp.where(mask, lm, 0.0).astype(BF16), dh) for lm, dh in zip(lmats, d16)]
        dinvs = [dv - _dot(dh, m.astype(BF16)) for dv, dh, m in zip(dinvs, d16, cd)]
        level += 1
    resid = [eye - dv - _mm3(_split_bf16(lm), _split_bf16(dv)) for lm, dv in zip(lmats, dinvs)]
    return [dv + _dot(dv.astype(BF16), r.astype(BF16)) for dv, r in zip(dinvs, resid)]


def _gdn_kernel(q_ref, k_ref, v_ref, z_ref, a_ref, b_ref, alog_ref, dtb_ref, onorm_ref, o_ref, *, seq):
    c = GDN_GROUP
    d = HEAD_DIM
    heads = v_ref.shape[1] // d
    ngrp = GDN_GROUPS_PER_STEP
    ii = lax.broadcasted_iota(jnp.int32, (c, c), 0)
    jj = lax.broadcasted_iota(jnp.int32, (c, c), 1)
    lower = ii >= jj
    strict = ii > jj
    eye_b = ii == jj
    eye = eye_b.astype(F32)
    ij_xor = ii ^ jj
    scale = d ** -0.5

    def step(it, states):
        rows, qs, ks, kks, qks = [], [], [], [], []
        for gg in range(ngrp):
            r0 = pl.multiple_of((it * ngrp + gg) * c, c)
            q = q_ref[pl.ds(r0, c), :] * scale
            k = k_ref[pl.ds(r0, c), :]
            k16 = k.astype(BF16)
            rows.append(r0)
            qs.append(q)
            ks.append(k)
            kks.append(_dot_nt(k16, k16))
            qks.append(_dot_nt(q.astype(BF16), k16))
        prob = []
        for gg in range(ngrp):
            for h in range(heads):
                a_row = a_ref[h:h + 1, pl.ds(rows[gg], c)]
                b_row = b_ref[h:h + 1, pl.ds(rows[gg], c)]
                x = a_row + dtb_ref[h:h + 1, :]
                softplus = jnp.maximum(x, 0.0) + jnp.log1p(jnp.exp(-jnp.abs(x)))
                g_row = -jnp.exp(alog_ref[h:h + 1, :]) * softplus
                beta_row = _sigmoid(b_row)
                gc_col = jnp.sum(jnp.where(lower, g_row, 0.0), axis=1, keepdims=True)
                gc_row = jnp.sum(jnp.where(eye_b, gc_col, 0.0), axis=0, keepdims=True)
                beta_col = jnp.sum(jnp.where(eye_b, beta_row, 0.0), axis=1, keepdims=True)
                decay = jnp.where(lower, jnp.exp(jnp.where(lower, gc_col - gc_row, 0.0)), 0.0)
                lmat = jnp.where(strict, beta_col * kks[gg] * decay, 0.0)
                prob.append(dict(gg=gg, h=h, gc_col=gc_col, beta_col=beta_col, decay=decay, lmat=lmat))
        tinv = _unit_lower_inverses([p["lmat"] for p in prob], eye, ij_xor)
        for p, t in zip(prob, tinv):
            gg, h, gc_col = p["gg"], p["h"], p["gc_col"]
            v = v_ref[pl.ds(rows[gg], c), h * d:(h + 1) * d]
            rhs = p["beta_col"] * jnp.concatenate([v, ks[gg] * jnp.exp(gc_col)], axis=1)
            uw = _dot(t.astype(BF16), rhs.astype(BF16))
            p["u"] = uw[:, :d]
            p["w16"] = uw[:, d:].astype(BF16)
            p["attn16"] = jnp.where(lower, qks[gg] * p["decay"], 0.0).astype(BF16)
            p["qg16"] = (qs[gg] * jnp.exp(gc_col)).astype(BF16)
            g_last = gc_col[c - 1:c, :]
            p["kdec16"] = (ks[gg] * jnp.exp(g_last - gc_col)).astype(BF16)
            p["carry"] = jnp.exp(g_last)
        states = list(states)
        for p in prob:
            gg, h = p["gg"], p["h"]
            s = states[h]
            s16 = s.astype(BF16)
            v_new = p["u"] - _dot(p["w16"], s16)
            vn16 = v_new.astype(BF16)
            o = _dot(p["qg16"], s16) + _dot(p["attn16"], vn16)
            states[h] = s * p["carry"] + _dot_tn(p["kdec16"], vn16)
            ms = jnp.mean(o * o, axis=-1, keepdims=True)
            z = z_ref[pl.ds(rows[gg], c), h * d:(h + 1) * d]
            y = o * lax.rsqrt(ms + NORM_EPS) * onorm_ref[...] * (z * _sigmoid(z))
            o_ref[pl.ds(rows[gg], c), h * d:(h + 1) * d] = y.astype(o_ref.dtype)
        return tuple(states)

    init = tuple(jnp.zeros((d, d), F32) for _ in range(heads))
    lax.fori_loop(0, seq // (c * ngrp), step, init)


def gdn_core(qk_act, v_act, proj, z_off, a_t, b_t, alog, dtb, o_norm, batch, seq):
    d = HEAD_DIM
    rep = GDN_V_HEADS // GDN_QK_HEADS
    m = qk_act.shape[0]
    zb = z_off // (rep * d)
    assert seq % (GDN_GROUP * GDN_GROUPS_PER_STEP) == 0
    return pl.pallas_call(
        functools.partial(_gdn_kernel, seq=seq),
        grid=(batch, GDN_QK_HEADS),
        in_specs=[pl.BlockSpec((seq, d), lambda b, j: (b, j)),
                  pl.BlockSpec((seq, d), lambda b, j: (b, GDN_QK_HEADS + j)),
                  pl.BlockSpec((seq, rep * d), lambda b, j: (b, j)),
                  pl.BlockSpec((seq, rep * d), lambda b, j: (b, zb + j)),
                  pl.BlockSpec((None, None, rep, seq), lambda b, j: (b, j, 0, 0)),
                  pl.BlockSpec((None, None, rep, seq), lambda b, j: (b, j, 0, 0)),
                  pl.BlockSpec((None, rep, GDN_GROUP), lambda b, j: (j, 0, 0)),
                  pl.BlockSpec((None, rep, GDN_GROUP), lambda b, j: (j, 0, 0)),
                  pl.BlockSpec((1, d), lambda b, j: (0, 0))],
        out_specs=pl.BlockSpec((seq, rep * d), lambda b, j: (b, j)),
        out_shape=jax.ShapeDtypeStruct((m, GDN_V_HEADS * d), BF16),
        compiler_params=_params("parallel", "parallel"),
        name="gdn_core",
    )(qk_act, qk_act, v_act, proj, a_t, b_t, alog, dtb, o_norm.reshape(1, d))


def _moba_prep_kernel(x_ref, gain_ref, cos_ref, sin_ref, o_ref):
    d = HEAD_DIM
    for h in range(x_ref.shape[1] // d):
        x = x_ref[:, h * d:(h + 1) * d]
        ms = jnp.mean(x * x, axis=-1, keepdims=True)
        y = x * lax.rsqrt(ms + NORM_EPS) * gain_ref[...]
        o_ref[:, h * d:(h + 1) * d] = y * cos_ref[...] + pltpu.roll(y, d // 2, axis=1) * sin_ref[...]


def moba_prep(proj, col_off, width, gain, cos_t, sin_t, seq, ts=512, tc=512):
    m = proj.shape[0]
    d = HEAD_DIM
    cb = col_off // tc
    tps = seq // ts
    return pl.pallas_call(
        _moba_prep_kernel,
        grid=(m // ts, width // tc),
        in_specs=[pl.BlockSpec((ts, tc), lambda i, j: (i, cb + j)),
                  pl.BlockSpec((1, d), lambda i, j: (0, 0)),
                  pl.BlockSpec((ts, d), lambda i, j: (i % tps, 0)),
                  pl.BlockSpec((ts, d), lambda i, j: (i % tps, 0))],
        out_specs=pl.BlockSpec((ts, tc), lambda i, j: (i, j)),
        out_shape=jax.ShapeDtypeStruct((m, width), F32),
        compiler_params=_params("parallel", "parallel"),
        name="moba_prep",
    )(proj, gain.reshape(1, d), cos_t, sin_t)


def _moba_kernel(q_ref, k_ref, v_ref, o_ref, kpad_ref, vt_ref, kmean_ref, sel_ref, *, nblk):
    blk = MOBA_BLOCK
    d = HEAD_DIM
    cb = MOBA_CHUNK
    pad = cb - 1
    heads = q_ref.shape[1] // d
    scale = d ** -0.5
    sel0 = SUBLANES

    for h in range(heads):
        kpad_ref[h, 0:pad * blk, :] = jnp.zeros((pad * blk, d), BF16)
        vt_ref[h, 0:pad] = jnp.zeros((pad, d, blk), BF16)
        sel_ref[h, 0:sel0, :] = jnp.zeros((sel0, blk), F32)
        for n in range(nblk):
            kb = k_ref[n * blk:(n + 1) * blk, h * d:(h + 1) * d]
            kmean_ref[h, n:n + 1, :] = jnp.mean(kb, axis=0, keepdims=True)
            kpad_ref[h, (pad + n) * blk:(pad + n + 1) * blk, :] = kb.astype(BF16)
            vt_ref[h, pad + n] = v_ref[n * blk:(n + 1) * blk, h * d:(h + 1) * d].T.astype(BF16)

    nidx = lax.broadcasted_iota(jnp.int32, (nblk, blk), 0)
    kpos = lax.broadcasted_iota(jnp.int32, (blk, blk), 0)
    qpos = lax.broadcasted_iota(jnp.int32, (blk, blk), 1)
    causal = kpos <= qpos

    def chunk_scores(h, q16, first_blk, own):
        k0 = pl.multiple_of((first_blk + pad) * blk, blk)
        st = _dot_nt(kpad_ref[h, pl.ds(k0, cb * blk), :], q16)
        parts = []
        for u in range(cb):
            su = st[u * blk:(u + 1) * blk, :]
            if own and u == cb - 1:
                keep = causal
            else:
                keep = sel_ref[h, pl.ds(sel0 + first_blk + u, 1), :] > 0.0
            parts.append(jnp.where(keep, su, -jnp.inf))
        return parts

    def chunk_pv(h, first_blk, p_parts):
        acc = None
        for u in range(cb):
            term = _dot(vt_ref[h, first_blk + pad + u], p_parts[u].astype(BF16))
            acc = term if acc is None else acc + term
        return acc

    def qblock(qb, carry):
        r0 = pl.multiple_of(qb * blk, blk)
        q16s, stats = [], []
        for h in range(heads):
            q = q_ref[pl.ds(r0, blk), h * d:(h + 1) * d]
            gate = lax.dot_general(kmean_ref[h], q, (((1,), (1,)), ((), ())),
                                   precision=lax.Precision.HIGHEST, preferred_element_type=F32)
            rank = jnp.zeros((nblk, blk), jnp.int32)
            for m in range(nblk):
                gm = gate[m:m + 1, :]
                beats = (gm > gate) | ((gm == gate) & (m < nidx))
                rank = rank + jnp.where(beats & (m < qb), 1, 0)
            sel_ref[h, sel0:sel0 + nblk, :] = jnp.where((nidx < qb) & (rank < MOBA_TOP_K), 1.0, 0.0)
            q16 = (q * scale).astype(BF16)
            q16s.append(q16)
            parts = chunk_scores(h, q16, qb - pad, own=True)
            m0 = functools.reduce(jnp.maximum, [jnp.max(s, axis=0, keepdims=True) for s in parts])
            p_parts = [jnp.exp(s - m0) for s in parts]
            l0 = functools.reduce(jnp.add, [jnp.sum(p, axis=0, keepdims=True) for p in p_parts])
            stats.append((m0, l0, chunk_pv(h, qb - pad, p_parts)))

        def older(t, st):
            first_blk = qb - pad - cb * t
            new = []
            for h in range(heads):
                m_i, l_i, acc = st[h]
                parts = chunk_scores(h, q16s[h], first_blk, own=False)
                m_new = functools.reduce(jnp.maximum, [m_i] + [jnp.max(s, axis=0, keepdims=True) for s in parts])
                alpha = jnp.exp(m_i - m_new)
                p_parts = [jnp.exp(s - m_new) for s in parts]
                l_new = alpha * l_i + functools.reduce(jnp.add, [jnp.sum(p, axis=0, keepdims=True) for p in p_parts])
                new.append((m_new, l_new, alpha * acc + chunk_pv(h, first_blk, p_parts)))
            return tuple(new)

        final = lax.fori_loop(1, qb // cb + 1, older, tuple(stats))
        for h in range(heads):
            _, l_f, acc_f = final[h]
            o_ref[pl.ds(r0, blk), h * d:(h + 1) * d] = (acc_f / l_f).T.astype(o_ref.dtype)
        return carry

    lax.fori_loop(0, nblk, qblock, 0)


def moba_core(q_r, k_r, proj, v_off, batch, seq):
    d = HEAD_DIM
    hp = MOBA_HEADS_PER_STEP
    m = q_r.shape[0]
    nblk = seq // MOBA_BLOCK
    pad = MOBA_CHUNK - 1
    vb = v_off // (hp * d)
    return pl.pallas_call(
        functools.partial(_moba_kernel, nblk=nblk),
        grid=(batch, MOBA_HEADS // hp),
        in_specs=[pl.BlockSpec((seq, hp * d), lambda b, h: (b, h)),
                  pl.BlockSpec((seq, hp * d), lambda b, h: (b, h)),
                  pl.BlockSpec((seq, hp * d), lambda b, h: (b, vb + h))],
        out_specs=pl.BlockSpec((seq, hp * d), lambda b, h: (b, h)),
        out_shape=jax.ShapeDtypeStruct((m, MOBA_HEADS * d), BF16),
        scratch_shapes=[pltpu.VMEM((hp, (nblk + pad) * MOBA_BLOCK, d), BF16),
                        pltpu.VMEM((hp, nblk + pad, d, MOBA_BLOCK), BF16),
                        pltpu.VMEM((hp, nblk, d), F32),
                        pltpu.VMEM((hp, SUBLANES + nblk, MOBA_BLOCK), F32)],
        compiler_params=_params("parallel", "parallel"),
        name="moba_core",
    )(q_r, k_r, proj)


def _swiglu_half_step(x, norm, w_gate, w_up, w_down):
    h = rmsnorm_bf16(x, norm)
    act = ffn_up(h, w_gate.astype(BF16), w_up.astype(BF16))
    return matmul_residual(act, w_down.astype(BF16), x, 0.5, tm=512, tn=256)


def _rope_tables(seq):
    half = HEAD_DIM // 2
    inv_freq = ROPE_THETA ** (-jnp.arange(half, dtype=F32) / half)
    ang = jnp.arange(seq).astype(F32)[:, None] * inv_freq[None, :]
    cos, sin = jnp.cos(ang), jnp.sin(ang)
    return jnp.concatenate([cos, cos], axis=1), jnp.concatenate([-sin, sin], axis=1)


def _layer(x, ffn1_norm, ffn1_w_gate, ffn1_w_up, ffn1_w_down, mix_norm, w_in, gdn_conv_w, gdn_A_log,
           gdn_dt_bias, gdn_o_norm, moba_q_norm, moba_k_norm, w_branch_gdn, w_branch_moba, w_out,
           ffn2_norm, ffn2_w_gate, ffn2_w_up, ffn2_w_down, batch, seq):
    d = HEAD_DIM
    qkw = GDN_QK_HEADS * d
    vw = GDN_V_HEADS * d
    mw = MOBA_HEADS * d
    dm = x.shape[1]

    x = _swiglu_half_step(x, ffn1_norm, ffn1_w_gate, ffn1_w_up, ffn1_w_down)

    ab_off = 2 * qkw + 2 * vw
    w_main = jnp.concatenate([w_in[:, :ab_off], w_in[:, ab_off + 2 * GDN_V_HEADS:]], axis=1).astype(BF16)
    w_ab = jnp.pad(w_in[:, ab_off:ab_off + 2 * GDN_V_HEADS], ((0, 0), (0, d - 2 * GDN_V_HEADS))).astype(BF16)
    q_off, v_off, z_off = 0, 2 * qkw, 2 * qkw + vw
    mq_off = z_off + vw
    mk_off, mv_off = mq_off + mw, mq_off + 2 * mw
    gg_off = mq_off + 3 * mw
    gm_off = gg_off + dm

    h = rmsnorm_bf16(x, mix_norm)
    proj = matmul(h, w_main, F32)
    ab = matmul(h, w_ab, F32)

    rep = GDN_V_HEADS // GDN_QK_HEADS
    qk_act = gdn_prep(proj, gdn_conv_w[:, :2 * qkw], q_off, 2 * qkw, seq, l2=True)
    v_act = gdn_prep(proj, gdn_conv_w[:, 2 * qkw:], v_off, vw, seq, l2=False)
    to_heads = lambda t: t.reshape(batch, seq, GDN_QK_HEADS, rep).transpose(0, 2, 3, 1)
    a_t = to_heads(ab[:, :GDN_V_HEADS])
    b_t = to_heads(ab[:, GDN_V_HEADS:2 * GDN_V_HEADS])
    per_head = lambda t: jnp.broadcast_to(t.astype(F32).reshape(GDN_QK_HEADS, rep, 1), (GDN_QK_HEADS, rep, GDN_GROUP))
    o_gdn = gdn_core(qk_act, v_act, proj, z_off, a_t, b_t, per_head(gdn_A_log), per_head(gdn_dt_bias),
                     gdn_o_norm, batch, seq)

    cos_t, sin_t = _rope_tables(seq)
    q_r = moba_prep(proj, mq_off, mw, moba_q_norm, cos_t, sin_t, seq)
    k_r = moba_prep(proj, mk_off, mw, moba_k_norm, cos_t, sin_t, seq)
    o_moba = moba_core(q_r, k_r, proj, mv_off, batch, seq)

    y = merge_branches(o_gdn, o_moba, w_branch_gdn.astype(BF16), w_branch_moba.astype(BF16), proj, gg_off, gm_off)
    x = matmul_residual(y, w_out.astype(BF16), x, 1.0, tm=1024, tn=256)

    return _swiglu_half_step(x, ffn2_norm, ffn2_w_gate, ffn2_w_up, ffn2_w_down)


def kernel(x, ffn1_norm, ffn1_w_gate, ffn1_w_up, ffn1_w_down, mix_norm, w_in, gdn_conv_w, gdn_A_log, gdn_dt_bias,
           gdn_o_norm, moba_q_norm, moba_k_norm, w_branch_gdn, w_branch_moba, w_out, ffn2_norm, ffn2_w_gate,
           ffn2_w_up, ffn2_w_down):
    batch, seq, dm = x.shape
    assert ffn1_norm.shape[0] == 1, "single-layer stack"
    out = _layer(x.reshape(batch * seq, dm), ffn1_norm[0], ffn1_w_gate[0], ffn1_w_up[0], ffn1_w_down[0],
                 mix_norm[0], w_in[0], gdn_conv_w[0], gdn_A_log[0], gdn_dt_bias[0], gdn_o_norm[0],
                 moba_q_norm[0], moba_k_norm[0], w_branch_gdn[0], w_branch_moba[0], w_out[0],
                 ffn2_norm[0], ffn2_w_gate[0], ffn2_w_up[0], ffn2_w_down[0], batch, seq)
    return out.reshape(batch, seq, dm)
```

```python
import functools

import jax
import jax.numpy as jnp
from jax import lax
from jax.experimental import pallas as pl
from jax.experimental.pallas import tpu as pltpu

F32 = jnp.float32
BF16 = jnp.bfloat16

NORM_EPS = 1e-6
HEAD_DIM = 128
GDN_QK_HEADS = 16
GDN_V_HEADS = 32
GDN_CONV = 4
GDN_GROUP = 256
GDN_GROUPS_PER_STEP = 4
MOBA_HEADS = 16
MOBA_HEADS_PER_STEP = 2
MOBA_BLOCK = 256
MOBA_TOP_K = 3
MOBA_CHUNK = 4
ROPE_THETA = 10000.0
LOG2_E = 1.4426950408889634
CONV_HALO = 8
SUBLANES = 8

VMEM_LIMIT = 56 * 1024 * 1024


def _params(*sem):
    return pltpu.CompilerParams(dimension_semantics=sem, vmem_limit_bytes=VMEM_LIMIT)


def _sigmoid(x):
    return 1.0 / (1.0 + jnp.exp(-x))


def _dot(a, b):
    return jnp.dot(a, b, preferred_element_type=F32)


def _dot_nt(a, b):
    return lax.dot_general(a, b, (((1,), (1,)), ((), ())), preferred_element_type=F32)


def _dot_tn(a, b):
    return lax.dot_general(a, b, (((0,), (0,)), ((), ())), preferred_element_type=F32)


def _rmsnorm_kernel(x_ref, g_ref, o_ref):
    x = x_ref[...]
    ms = jnp.mean(x * x, axis=-1, keepdims=True)
    o_ref[...] = (x * lax.rsqrt(ms + NORM_EPS) * g_ref[...]).astype(o_ref.dtype)


def rmsnorm_bf16(x, gain, tm=256):
    m, d = x.shape
    return pl.pallas_call(
        _rmsnorm_kernel,
        grid=(m // tm,),
        in_specs=[pl.BlockSpec((tm, d), lambda i: (i, 0)),
                  pl.BlockSpec((1, d), lambda i: (0, 0))],
        out_specs=pl.BlockSpec((tm, d), lambda i: (i, 0)),
        out_shape=jax.ShapeDtypeStruct((m, d), BF16),
        compiler_params=_params("parallel"),
        name="rmsnorm",
    )(x, gain.reshape(1, d))


def _ffn_up_kernel(h_ref, wg_ref, wu_ref, o_ref):
    h = h_ref[...]
    g = _dot(h, wg_ref[...])
    u = _dot(h, wu_ref[...])
    o_ref[...] = (g * _sigmoid(g) * u).astype(o_ref.dtype)


def ffn_up(h, wg, wu, tm=1024, tn=256):
    m, k = h.shape
    n = wg.shape[1]
    return pl.pallas_call(
        _ffn_up_kernel,
        grid=(m // tm, n // tn),
        in_specs=[pl.BlockSpec((tm, k), lambda i, j: (i, 0)),
                  pl.BlockSpec((k, tn), lambda i, j: (0, j)),
                  pl.BlockSpec((k, tn), lambda i, j: (0, j))],
        out_specs=pl.BlockSpec((tm, tn), lambda i, j: (i, j)),
        out_shape=jax.ShapeDtypeStruct((m, n), BF16),
        compiler_params=_params("parallel", "arbitrary"),
        name="ffn_up",
    )(h, wg, wu)


def _mm_kernel(a_ref, b_ref, o_ref):
    o_ref[...] = _dot(a_ref[...], b_ref[...]).astype(o_ref.dtype)


def matmul(a, b, out_dtype, tm=1024, tn=512):
    m, k = a.shape
    n = b.shape[1]
    tn = min(tn, n)
    return pl.pallas_call(
        _mm_kernel,
        grid=(m // tm, n // tn),
        in_specs=[pl.BlockSpec((tm, k), lambda i, j: (i, 0)),
                  pl.BlockSpec((k, tn), lambda i, j: (0, j))],
        out_specs=pl.BlockSpec((tm, tn), lambda i, j: (i, j)),
        out_shape=jax.ShapeDtypeStruct((m, n), out_dtype),
        compiler_params=_params("parallel", "arbitrary"),
        name="matmul",
    )(a, b)


def _mm_residual_kernel(a_ref, b_ref, r_ref, o_ref, *, scale):
    o_ref[...] = r_ref[...] + scale * _dot(a_ref[...], b_ref[...])


def matmul_residual(a, b, res, scale, tm, tn):
    m, k = a.shape
    n = b.shape[1]
    return pl.pallas_call(
        functools.partial(_mm_residual_kernel, scale=scale),
        grid=(m // tm, n // tn),
        in_specs=[pl.BlockSpec((tm, k), lambda i, j: (i, 0)),
                  pl.BlockSpec((k, tn), lambda i, j: (0, j)),
                  pl.BlockSpec((tm, tn), lambda i, j: (i, j))],
        out_specs=pl.BlockSpec((tm, tn), lambda i, j: (i, j)),
        out_shape=jax.ShapeDtypeStruct((m, n), F32),
        compiler_params=_params("parallel", "arbitrary"),
        name="matmul_residual",
    )(a, b, res)


def _merge_kernel(og_ref, om_ref, wg_ref, wm_ref, gg_ref, gm_ref, o_ref):
    yg = _dot(og_ref[...], wg_ref[...])
    ym = _dot(om_ref[...], wm_ref[...])
    o_ref[...] = (_sigmoid(gg_ref[...]) * yg + _sigmoid(gm_ref[...]) * ym).astype(o_ref.dtype)


def merge_branches(o_gdn, o_moba, wg, wm, proj, gg_off, gm_off, tm=1024, tn=512):
    m, kg = o_gdn.shape
    km = o_moba.shape[1]
    n = wg.shape[1]
    ggb, gmb = gg_off // tn, gm_off // tn
    return pl.pallas_call(
        _merge_kernel,
        grid=(m // tm, n // tn),
        in_specs=[pl.BlockSpec((tm, kg), lambda i, j: (i, 0)),
                  pl.BlockSpec((tm, km), lambda i, j: (i, 0)),
                  pl.BlockSpec((kg, tn), lambda i, j: (0, j)),
                  pl.BlockSpec((km, tn), lambda i, j: (0, j)),
                  pl.BlockSpec((tm, tn), lambda i, j: (i, ggb + j)),
                  pl.BlockSpec((tm, tn), lambda i, j: (i, gmb + j))],
        out_specs=pl.BlockSpec((tm, tn), lambda i, j: (i, j)),
        out_shape=jax.ShapeDtypeStruct((m, n), BF16),
        compiler_params=_params("parallel", "arbitrary"),
        name="merge_branches",
    )(o_gdn, o_moba, wg, wm, proj, proj)


def _split_bf16(a):
    hi = a.astype(BF16)
    lo = (a - hi.astype(F32)).astype(BF16)
    return hi, lo


def _mm3(a_split, b_split):
    ah, al = a_split
    bh, bl = b_split
    lhs = jnp.concatenate([ah, ah, al], axis=1)
    rhs = jnp.concatenate([bh, bl, bh], axis=0)
    return _dot(lhs, rhs)


def _unit_lower_inverses(lmats, eye, ij_xor):
    size = eye.shape[0]
    dinvs = [eye - jnp.where(ij_xor == 1, lm, 0.0) for lm in lmats]
    level = 1
    while (2 << level) <= size:
        mask = (ij_xor >> level) == 1
        d16 = [dv.astype(BF16) for dv in dinvs]
        cd = [_dot(jnp.where(mask, lm, 0.0).astype(BF16), dh) for lm, dh in zip(lmats, d16)]
        dinvs = [dv - _dot(dh, m.astype(BF16)) for dv, dh, m in zip(dinvs, d16, cd)]
        level += 1
    resid = [eye - dv - _mm3(_split_bf16(lm), _split_bf16(dv)) for lm, dv in zip(lmats, dinvs)]
    return [dv + _dot(dv.astype(BF16), r.astype(BF16)) for dv, r in zip(dinvs, resid)]


def _conv_silu(x_ref, w_ref, r0, rows):
    x = x_ref[pl.ds(r0, rows), :]
    prev = x_ref[pl.ds(pl.multiple_of(jnp.maximum(r0 - CONV_HALO, 0), CONV_HALO), CONV_HALO), :]
    xx = jnp.concatenate([jnp.where(r0 > 0, prev, 0.0), x], axis=0)
    acc = x * w_ref[GDN_CONV - 1:GDN_CONV, :]
    for j in range(GDN_CONV - 1):
        acc = acc + pltpu.roll(xx, GDN_CONV - 1 - j, axis=0)[CONV_HALO:CONV_HALO + rows] * w_ref[j:j + 1, :]
    return acc * _sigmoid(acc)


def _l2_normalize(x):
    return x * lax.rsqrt(jnp.sum(x * x, axis=-1, keepdims=True) + NORM_EPS)


def _gdn_kernel(q_ref, k_ref, v_ref, z_ref, wq_ref, wk_ref, wv_ref, a_ref, b_ref, alog_ref, dtb_ref, onorm_ref,
                o_ref, *, seq):
    c = GDN_GROUP
    d = HEAD_DIM
    heads = v_ref.shape[1] // d
    ngrp = GDN_GROUPS_PER_STEP
    ii = lax.broadcasted_iota(jnp.int32, (c, c), 0)
    jj = lax.broadcasted_iota(jnp.int32, (c, c), 1)
    lower = ii >= jj
    strict = ii > jj
    eye_b = ii == jj
    eye = eye_b.astype(F32)
    ij_xor = ii ^ jj
    scale = d ** -0.5

    def step(it, states):
        rows, qs, ks, vs, kks, qks = [], [], [], [], [], []
        for gg in range(ngrp):
            r0 = pl.multiple_of((it * ngrp + gg) * c, c)
            q = _l2_normalize(_conv_silu(q_ref, wq_ref, r0, c)) * scale
            k = _l2_normalize(_conv_silu(k_ref, wk_ref, r0, c))
            k16 = k.astype(BF16)
            rows.append(r0)
            qs.append(q)
            ks.append(k)
            vs.append(_conv_silu(v_ref, wv_ref, r0, c))
            kks.append(_dot_nt(k16, k16))
            qks.append(_dot_nt(q.astype(BF16), k16))
        prob = []
        for gg in range(ngrp):
            for h in range(heads):
                a_row = a_ref[h:h + 1, pl.ds(rows[gg], c)]
                b_row = b_ref[h:h + 1, pl.ds(rows[gg], c)]
                x = a_row + dtb_ref[h:h + 1, :]
                softplus = jnp.maximum(x, 0.0) + jnp.log1p(jnp.exp(-jnp.abs(x)))
                g_row = -jnp.exp(alog_ref[h:h + 1, :]) * softplus
                beta_row = _sigmoid(b_row)
                gc_col = jnp.sum(jnp.where(lower, g_row, 0.0), axis=1, keepdims=True)
                gc_row = jnp.sum(jnp.where(eye_b, gc_col, 0.0), axis=0, keepdims=True)
                beta_col = jnp.sum(jnp.where(eye_b, beta_row, 0.0), axis=1, keepdims=True)
                decay = jnp.where(lower, jnp.exp(jnp.where(lower, gc_col - gc_row, 0.0)), 0.0)
                lmat = jnp.where(strict, beta_col * kks[gg] * decay, 0.0)
                prob.append(dict(gg=gg, h=h, gc_col=gc_col, beta_col=beta_col, decay=decay, lmat=lmat))
        tinv = _unit_lower_inverses([p["lmat"] for p in prob], eye, ij_xor)
        for p, t in zip(prob, tinv):
            gg, h, gc_col = p["gg"], p["h"], p["gc_col"]
            v = vs[gg][:, h * d:(h + 1) * d]
            rhs = p["beta_col"] * jnp.concatenate([v, ks[gg] * jnp.exp(gc_col)], axis=1)
            uw = _dot(t.astype(BF16), rhs.astype(BF16))
            p["u"] = uw[:, :d]
            p["w16"] = uw[:, d:].astype(BF16)
            p["attn16"] = jnp.where(lower, qks[gg] * p["decay"], 0.0).astype(BF16)
            p["qg16"] = (qs[gg] * jnp.exp(gc_col)).astype(BF16)
            g_last = gc_col[c - 1:c, :]
            p["kdec16"] = (ks[gg] * jnp.exp(g_last - gc_col)).astype(BF16)
            p["carry"] = jnp.exp(g_last)
        states = list(states)
        for p in prob:
            gg, h = p["gg"], p["h"]
            s = states[h]
            s16 = s.astype(BF16)
            v_new = p["u"] - _dot(p["w16"], s16)
            vn16 = v_new.astype(BF16)
            o = _dot(p["qg16"], s16) + _dot(p["attn16"], vn16)
            states[h] = s * p["carry"] + _dot_tn(p["kdec16"], vn16)
            ms = jnp.mean(o * o, axis=-1, keepdims=True)
            z = z_ref[pl.ds(rows[gg], c), h * d:(h + 1) * d]
            y = o * lax.rsqrt(ms + NORM_EPS) * onorm_ref[...] * (z * _sigmoid(z))
            o_ref[pl.ds(rows[gg], c), h * d:(h + 1) * d] = y.astype(o_ref.dtype)
        return tuple(states)

    init = tuple(jnp.zeros((d, d), F32) for _ in range(heads))
    lax.fori_loop(0, seq // (c * ngrp), step, init)


def gdn_core(proj, conv_w, a_t, b_t, alog, dtb, o_norm, batch, seq):
    d = HEAD_DIM
    rep = GDN_V_HEADS // GDN_QK_HEADS
    m = proj.shape[0]
    kb = GDN_QK_HEADS
    vb = 2 * GDN_QK_HEADS // rep
    zb = vb + GDN_V_HEADS // rep
    assert seq % (GDN_GROUP * GDN_GROUPS_PER_STEP) == 0
    return pl.pallas_call(
        functools.partial(_gdn_kernel, seq=seq),
        grid=(batch, GDN_QK_HEADS),
        in_specs=[pl.BlockSpec((seq, d), lambda b, j: (b, j)),
                  pl.BlockSpec((seq, d), lambda b, j: (b, kb + j)),
                  pl.BlockSpec((seq, rep * d), lambda b, j: (b, vb + j)),
                  pl.BlockSpec((seq, rep * d), lambda b, j: (b, zb + j)),
                  pl.BlockSpec((GDN_CONV, d), lambda b, j: (0, j)),
                  pl.BlockSpec((GDN_CONV, d), lambda b, j: (0, kb + j)),
                  pl.BlockSpec((GDN_CONV, rep * d), lambda b, j: (0, vb + j)),
                  pl.BlockSpec((None, None, rep, seq), lambda b, j: (b, j, 0, 0)),
                  pl.BlockSpec((None, None, rep, seq), lambda b, j: (b, j, 0, 0)),
                  pl.BlockSpec((None, rep, GDN_GROUP), lambda b, j: (j, 0, 0)),
                  pl.BlockSpec((None, rep, GDN_GROUP), lambda b, j: (j, 0, 0)),
                  pl.BlockSpec((1, d), lambda b, j: (0, 0))],
        out_specs=pl.BlockSpec((seq, rep * d), lambda b, j: (b, j)),
        out_shape=jax.ShapeDtypeStruct((m, GDN_V_HEADS * d), BF16),
        compiler_params=_params("parallel", "parallel"),
        name="gdn_core",
    )(proj, proj, proj, proj, conv_w, conv_w, conv_w, a_t, b_t, alog, dtb, o_norm.reshape(1, d))


def _norm_rope(x, gain, cos, sin):
    ms = jnp.mean(x * x, axis=-1, keepdims=True)
    y = x * lax.rsqrt(ms + NORM_EPS) * gain
    return y * cos + pltpu.roll(y, HEAD_DIM // 2, axis=1) * sin


def _moba_kernel(q_ref, k_ref, v_ref, qgain_ref, kgain_ref, cos_ref, sin_ref, o_ref,
                 kpad_ref, vt_ref, kmean_ref, sel_ref, *, nblk):
    blk = MOBA_BLOCK
    d = HEAD_DIM
    cb = MOBA_CHUNK
    pad = cb - 1
    heads = q_ref.shape[1] // d
    scale = d ** -0.5 * LOG2_E
    sel0 = SUBLANES

    for h in range(heads):
        kpad_ref[h, 0:pad * blk, :] = jnp.zeros((pad * blk, d), BF16)
        vt_ref[h, 0:pad] = jnp.zeros((pad, d, blk), BF16)
        sel_ref[h, 0:sel0, :] = jnp.zeros((sel0, blk), F32)
        for n in range(nblk):
            kb = _norm_rope(k_ref[n * blk:(n + 1) * blk, h * d:(h + 1) * d], kgain_ref[...],
                            cos_ref[n * blk:(n + 1) * blk, :], sin_ref[n * blk:(n + 1) * blk, :])
            kmean_ref[h, n:n + 1, :] = jnp.mean(kb, axis=0, keepdims=True)
            kpad_ref[h, (pad + n) * blk:(pad + n + 1) * blk, :] = kb.astype(BF16)
            vt_ref[h, pad + n] = v_ref[n * blk:(n + 1) * blk, h * d:(h + 1) * d].T.astype(BF16)

    nidx = lax.broadcasted_iota(jnp.int32, (nblk, blk), 0)
    kpos = lax.broadcasted_iota(jnp.int32, (blk, blk), 0)
    qpos = lax.broadcasted_iota(jnp.int32, (blk, blk), 1)
    causal = kpos <= qpos

    def chunk_scores(h, q16, first_blk, own):
        k0 = pl.multiple_of((first_blk + pad) * blk, blk)
        st = _dot_nt(kpad_ref[h, pl.ds(k0, cb * blk), :], q16)
        parts = []
        for u in range(cb):
            su = st[u * blk:(u + 1) * blk, :]
            if own and u == cb - 1:
                keep = causal
            else:
                keep = sel_ref[h, pl.ds(sel0 + first_blk + u, 1), :] > 0.0
            parts.append(jnp.where(keep, su, -jnp.inf))
        return parts

    def chunk_pv(h, first_blk, p_parts):
        acc = None
        for u in range(cb):
            term = _dot(vt_ref[h, first_blk + pad + u], p_parts[u].astype(BF16))
            acc = term if acc is None else acc + term
        return acc

    def qblock(qb, carry):
        r0 = pl.multiple_of(qb * blk, blk)
        hs = range(heads)
        cos_q, sin_q = cos_ref[pl.ds(r0, blk), :], sin_ref[pl.ds(r0, blk), :]
        qs = [_norm_rope(q_ref[pl.ds(r0, blk), h * d:(h + 1) * d], qgain_ref[...], cos_q, sin_q) for h in hs]
        q16s = [(q * scale).astype(BF16) for q in qs]
        gates = [lax.dot_general(kmean_ref[h], qs[h], (((1,), (1,)), ((), ())),
                                 precision=lax.Precision.HIGHEST, preferred_element_type=F32) for h in hs]
        for h in hs:
            rank = jnp.zeros((nblk, blk), jnp.int32)
            for m in range(nblk):
                gm = gates[h][m:m + 1, :]
                beats = (gm > gates[h]) | ((gm == gates[h]) & (m < nidx))
                rank = rank + jnp.where(beats & (m < qb), 1, 0)
            sel_ref[h, sel0:sel0 + nblk, :] = jnp.where((nidx < qb) & (rank < MOBA_TOP_K), 1.0, 0.0)
        parts0 = [chunk_scores(h, q16s[h], qb - pad, own=True) for h in hs]
        m0 = [functools.reduce(jnp.maximum, [jnp.max(s, axis=0, keepdims=True) for s in parts0[h]]) for h in hs]
        p0 = [[jnp.exp2(s - m0[h]) for s in parts0[h]] for h in hs]
        l0 = [functools.reduce(jnp.add, [jnp.sum(p, axis=0, keepdims=True) for p in p0[h]]) for h in hs]
        stats = [(m0[h], l0[h], chunk_pv(h, qb - pad, p0[h])) for h in hs]

        def older(t, st):
            first_blk = qb - pad - cb * t
            hs = range(heads)
            parts = [chunk_scores(h, q16s[h], first_blk, own=False) for h in hs]
            m_new = [functools.reduce(jnp.maximum, [st[h][0]] + [jnp.max(s, axis=0, keepdims=True) for s in parts[h]])
                     for h in hs]
            alpha = [jnp.exp2(st[h][0] - m_new[h]) for h in hs]
            p_parts = [[jnp.exp2(s - m_new[h]) for s in parts[h]] for h in hs]
            l_new = [alpha[h] * st[h][1] + functools.reduce(jnp.add, [jnp.sum(p, axis=0, keepdims=True) for p in p_parts[h]])
                     for h in hs]
            acc = [alpha[h] * st[h][2] + chunk_pv(h, first_blk, p_parts[h]) for h in hs]
            return tuple((m_new[h], l_new[h], acc[h]) for h in hs)

        final = lax.fori_loop(1, qb // cb + 1, older, tuple(stats))
        for h in range(heads):
            _, l_f, acc_f = final[h]
            o_ref[pl.ds(r0, blk), h * d:(h + 1) * d] = (acc_f / l_f).T.astype(o_ref.dtype)
        return carry

    lax.fori_loop(0, nblk, qblock, 0)


def moba_core(proj, q_off, k_off, v_off, q_gain, k_gain, cos_t, sin_t, batch, seq):
    d = HEAD_DIM
    hp = MOBA_HEADS_PER_STEP
    m = proj.shape[0]
    nblk = seq // MOBA_BLOCK
    pad = MOBA_CHUNK - 1
    qb, kb, vb = q_off // (hp * d), k_off // (hp * d), v_off // (hp * d)
    return pl.pallas_call(
        functools.partial(_moba_kernel, nblk=nblk),
        grid=(batch, MOBA_HEADS // hp),
        in_specs=[pl.BlockSpec((seq, hp * d), lambda b, h: (b, qb + h)),
                  pl.BlockSpec((seq, hp * d), lambda b, h: (b, kb + h)),
                  pl.BlockSpec((seq, hp * d), lambda b, h: (b, vb + h)),
                  pl.BlockSpec((1, d), lambda b, h: (0, 0)),
                  pl.BlockSpec((1, d), lambda b, h: (0, 0)),
                  pl.BlockSpec((seq, d), lambda b, h: (0, 0)),
                  pl.BlockSpec((seq, d), lambda b, h: (0, 0))],
        out_specs=pl.BlockSpec((seq, hp * d), lambda b, h: (b, h)),
        out_shape=jax.ShapeDtypeStruct((m, MOBA_HEADS * d), BF16),
        scratch_shapes=[pltpu.VMEM((hp, (nblk + pad) * MOBA_BLOCK, d), BF16),
                        pltpu.VMEM((hp, nblk + pad, d, MOBA_BLOCK), BF16),
                        pltpu.VMEM((hp, nblk, d), F32),
                        pltpu.VMEM((hp, SUBLANES + nblk, MOBA_BLOCK), F32)],
        compiler_params=_params("parallel", "parallel"),
        name="moba_core",
    )(proj, proj, proj, q_gain.reshape(1, d), k_gain.reshape(1, d), cos_t, sin_t)


def _swiglu_half_step(x, norm, w_gate, w_up, w_down):
    h = rmsnorm_bf16(x, norm)
    act = ffn_up(h, w_gate.astype(BF16), w_up.astype(BF16))
    return matmul_residual(act, w_down.astype(BF16), x, 0.5, tm=512, tn=512)


def _rope_tables(seq):
    half = HEAD_DIM // 2
    inv_freq = ROPE_THETA ** (-jnp.arange(half, dtype=F32) / half)
    ang = jnp.arange(seq).astype(F32)[:, None] * inv_freq[None, :]
    cos, sin = jnp.cos(ang), jnp.sin(ang)
    return jnp.concatenate([cos, cos], axis=1), jnp.concatenate([-sin, sin], axis=1)


def _layer(x, ffn1_norm, ffn1_w_gate, ffn1_w_up, ffn1_w_down, mix_norm, w_in, gdn_conv_w, gdn_A_log,
           gdn_dt_bias, gdn_o_norm, moba_q_norm, moba_k_norm, w_branch_gdn, w_branch_moba, w_out,
           ffn2_norm, ffn2_w_gate, ffn2_w_up, ffn2_w_down, batch, seq):
    d = HEAD_DIM
    qkw = GDN_QK_HEADS * d
    vw = GDN_V_HEADS * d
    mw = MOBA_HEADS * d
    dm = x.shape[1]

    x = _swiglu_half_step(x, ffn1_norm, ffn1_w_gate, ffn1_w_up, ffn1_w_down)

    ab_off = 2 * qkw + 2 * vw
    w_gdn = w_in[:, :ab_off].astype(BF16)
    w_ab = jnp.pad(w_in[:, ab_off:ab_off + 2 * GDN_V_HEADS], ((0, 0), (0, d - 2 * GDN_V_HEADS))).astype(BF16)
    w_rest = w_in[:, ab_off + 2 * GDN_V_HEADS:].astype(BF16)
    mq_off, mk_off, mv_off, gg_off, gm_off = 0, mw, 2 * mw, 3 * mw, 3 * mw + dm

    h = rmsnorm_bf16(x, mix_norm)
    proj_g = matmul(h, w_gdn, F32)
    ab = matmul(h, w_ab, F32)
    proj_m = matmul(h, w_rest, F32)

    rep = GDN_V_HEADS // GDN_QK_HEADS
    to_heads = lambda t: t.reshape(batch, seq, GDN_QK_HEADS, rep).transpose(0, 2, 3, 1)
    a_t = to_heads(ab[:, :GDN_V_HEADS])
    b_t = to_heads(ab[:, GDN_V_HEADS:2 * GDN_V_HEADS])
    per_head = lambda t: jnp.broadcast_to(t.astype(F32).reshape(GDN_QK_HEADS, rep, 1), (GDN_QK_HEADS, rep, GDN_GROUP))
    o_gdn = gdn_core(proj_g, gdn_conv_w, a_t, b_t, per_head(gdn_A_log), per_head(gdn_dt_bias), gdn_o_norm, batch, seq)

    cos_t, sin_t = _rope_tables(seq)
    o_moba = moba_core(proj_m, mq_off, mk_off, mv_off, moba_q_norm, moba_k_norm, cos_t, sin_t, batch, seq)

    y = merge_branches(o_gdn, o_moba, w_branch_gdn.astype(BF16), w_branch_moba.astype(BF16), proj_m, gg_off, gm_off)
    x = matmul_residual(y, w_out.astype(BF16), x, 1.0, tm=1024, tn=512)

    return _swiglu_half_step(x, ffn2_norm, ffn2_w_gate, ffn2_w_up, ffn2_w_down)


def kernel(x, ffn1_norm, ffn1_w_gate, ffn1_w_up, ffn1_w_down, mix_norm, w_in, gdn_conv_w, gdn_A_log, gdn_dt_bias,
           gdn_o_norm, moba_q_norm, moba_k_norm, w_branch_gdn, w_branch_moba, w_out, ffn2_norm, ffn2_w_gate,
           ffn2_w_up, ffn2_w_down):
    batch, seq, dm = x.shape
    assert ffn1_norm.shape[0] == 1, "single-layer stack"
    out = _layer(x.reshape(batch * seq, dm), ffn1_norm[0], ffn1_w_gate[0], ffn1_w_up[0], ffn1_w_down[0],
                 mix_norm[0], w_in[0], gdn_conv_w[0], gdn_A_log[0], gdn_dt_bias[0], gdn_o_norm[0],
                 moba_q_norm[0], moba_k_norm[0], w_branch_gdn[0], w_branch_moba[0], w_out[0],
                 ffn2_norm[0], ffn2_w_gate[0], ffn2_w_up[0], ffn2_w_down[0], batch, seq)
    return out.reshape(batch, seq, dm)
```

```python
import functools

import jax
import jax.numpy as jnp
from jax import lax
from jax.experimental import pallas as pl
from jax.experimental.pallas import tpu as pltpu

F32 = jnp.float32
BF16 = jnp.bfloat16

NORM_EPS = 1e-6
HEAD_DIM = 128
GDN_QK_HEADS = 16
GDN_V_HEADS = 32
GDN_CONV = 4
GDN_GROUP = 256
GDN_GROUPS_PER_STEP = 4
MOBA_HEADS = 16
MOBA_HEADS_PER_STEP = 2
MOBA_BLOCK = 256
MOBA_TOP_K = 3
MOBA_CHUNK = 4
ROPE_THETA = 10000.0
LOG2_E = 1.4426950408889634
CONV_HALO = 8
SUBLANES = 8

VMEM_LIMIT = 56 * 1024 * 1024


def _params(*sem):
    return pltpu.CompilerParams(dimension_semantics=sem, vmem_limit_bytes=VMEM_LIMIT)


def _sigmoid(x):
    return 1.0 / (1.0 + jnp.exp(-x))


def _dot(a, b):
    return jnp.dot(a, b, preferred_element_type=F32)


def _dot_nt(a, b):
    return lax.dot_general(a, b, (((1,), (1,)), ((), ())), preferred_element_type=F32)


def _dot_tn(a, b):
    return lax.dot_general(a, b, (((0,), (0,)), ((), ())), preferred_element_type=F32)


def _rmsnorm_kernel(x_ref, g_ref, o_ref):
    x = x_ref[...]
    ms = jnp.mean(x * x, axis=-1, keepdims=True)
    o_ref[...] = (x * lax.rsqrt(ms + NORM_EPS) * g_ref[...]).astype(o_ref.dtype)


def rmsnorm_bf16(x, gain, tm=256):
    m, d = x.shape
    return pl.pallas_call(
        _rmsnorm_kernel,
        grid=(m // tm,),
        in_specs=[pl.BlockSpec((tm, d), lambda i: (i, 0)),
                  pl.BlockSpec((1, d), lambda i: (0, 0))],
        out_specs=pl.BlockSpec((tm, d), lambda i: (i, 0)),
        out_shape=jax.ShapeDtypeStruct((m, d), BF16),
        compiler_params=_params("parallel"),
        name="rmsnorm",
    )(x, gain.reshape(1, d))


def _ffn_up_kernel(h_ref, wg_ref, wu_ref, o_ref):
    h = h_ref[...]
    g = _dot(h, wg_ref[...].astype(BF16))
    u = _dot(h, wu_ref[...].astype(BF16))
    o_ref[...] = (g * _sigmoid(g) * u).astype(o_ref.dtype)


def ffn_up(h, wg, wu, tm=1024, tn=256):
    m, k = h.shape
    n = wg.shape[1]
    return pl.pallas_call(
        _ffn_up_kernel,
        grid=(m // tm, n // tn),
        in_specs=[pl.BlockSpec((tm, k), lambda i, j: (i, 0)),
                  pl.BlockSpec((k, tn), lambda i, j: (0, j)),
                  pl.BlockSpec((k, tn), lambda i, j: (0, j))],
        out_specs=pl.BlockSpec((tm, tn), lambda i, j: (i, j)),
        out_shape=jax.ShapeDtypeStruct((m, n), BF16),
        compiler_params=_params("parallel", "arbitrary"),
        name="ffn_up",
    )(h, wg, wu)


def _mm_kernel(a_ref, b_ref, o_ref):
    o_ref[...] = _dot(a_ref[...], b_ref[...].astype(BF16)).astype(o_ref.dtype)


def matmul(a, b, n, out_dtype, tm=1024, tn=512):
    m, k = a.shape
    tn = min(tn, n)
    return pl.pallas_call(
        _mm_kernel,
        grid=(m // tm, n // tn),
        in_specs=[pl.BlockSpec((tm, k), lambda i, j: (i, 0)),
                  pl.BlockSpec((k, tn), lambda i, j: (0, j))],
        out_specs=pl.BlockSpec((tm, tn), lambda i, j: (i, j)),
        out_shape=jax.ShapeDtypeStruct((m, n), out_dtype),
        compiler_params=_params("parallel", "arbitrary"),
        name="matmul",
    )(a, b)


def _mm_residual_kernel(a_ref, b_ref, r_ref, o_ref, *, scale):
    o_ref[...] = r_ref[...] + scale * _dot(a_ref[...], b_ref[...])


def matmul_residual(a, b, res, scale, tm, tn):
    m, k = a.shape
    n = b.shape[1]
    return pl.pallas_call(
        functools.partial(_mm_residual_kernel, scale=scale),
        grid=(m // tm, n // tn),
        in_specs=[pl.BlockSpec((tm, k), lambda i, j: (i, 0)),
                  pl.BlockSpec((k, tn), lambda i, j: (0, j)),
                  pl.BlockSpec((tm, tn), lambda i, j: (i, j))],
        out_specs=pl.BlockSpec((tm, tn), lambda i, j: (i, j)),
        out_shape=jax.ShapeDtypeStruct((m, n), F32),
        compiler_params=_params("parallel", "arbitrary"),
        name="matmul_residual",
    )(a, b, res)


def _merge_kernel(og_ref, om_ref, wg_ref, wm_ref, gg_ref, gm_ref, o_ref):
    yg = _dot(og_ref[...], wg_ref[...])
    ym = _dot(om_ref[...], wm_ref[...])
    o_ref[...] = (_sigmoid(gg_ref[...]) * yg + _sigmoid(gm_ref[...]) * ym).astype(o_ref.dtype)


def merge_branches(o_gdn, o_moba, wg, wm, proj, gg_off, gm_off, tm=1024, tn=512):
    m, kg = o_gdn.shape
    km = o_moba.shape[1]
    n = wg.shape[1]
    ggb, gmb = gg_off // tn, gm_off // tn
    return pl.pallas_call(
        _merge_kernel,
        grid=(m // tm, n // tn),
        in_specs=[pl.BlockSpec((tm, kg), lambda i, j: (i, 0)),
                  pl.BlockSpec((tm, km), lambda i, j: (i, 0)),
                  pl.BlockSpec((kg, tn), lambda i, j: (0, j)),
                  pl.BlockSpec((km, tn), lambda i, j: (0, j)),
                  pl.BlockSpec((tm, tn), lambda i, j: (i, ggb + j)),
                  pl.BlockSpec((tm, tn), lambda i, j: (i, gmb + j))],
        out_specs=pl.BlockSpec((tm, tn), lambda i, j: (i, j)),
        out_shape=jax.ShapeDtypeStruct((m, n), BF16),
        compiler_params=_params("parallel", "arbitrary"),
        name="merge_branches",
    )(o_gdn, o_moba, wg, wm, proj, proj)


def _split_bf16(a):
    hi = a.astype(BF16)
    lo = (a - hi.astype(F32)).astype(BF16)
    return hi, lo


def _mm3(a_split, b_split):
    ah, al = a_split
    bh, bl = b_split
    lhs = jnp.concatenate([ah, ah, al], axis=1)
    rhs = jnp.concatenate([bh, bl, bh], axis=0)
    return _dot(lhs, rhs)


def _unit_lower_inverses(lmats, eye, ij_xor):
    size = eye.shape[0]
    dinvs = [eye - jnp.where(ij_xor == 1, lm, 0.0) for lm in lmats]
    level = 1
    while (2 << level) <= size:
        mask = (ij_xor >> level) == 1
        d16 = [dv.astype(BF16) for dv in dinvs]
        cd = [_dot(jnp.where(mask, lm, 0.0).astype(BF16), dh) for lm, dh in zip(lmats, d16)]
        dinvs = [dv - _dot(dh, m.astype(BF16)) for dv, dh, m in zip(dinvs, d16, cd)]
        level += 1
    resid = [eye - dv - _mm3(_split_bf16(lm), _split_bf16(dv)) for lm, dv in zip(lmats, dinvs)]
    return [dv + _dot(dv.astype(BF16), r.astype(BF16)) for dv, r in zip(dinvs, resid)]


def _conv_silu(x_ref, w_ref, r0, rows):
    x = x_ref[pl.ds(r0, rows), :]
    prev = x_ref[pl.ds(pl.multiple_of(jnp.maximum(r0 - CONV_HALO, 0), CONV_HALO), CONV_HALO), :]
    xx = jnp.concatenate([jnp.where(r0 > 0, prev, 0.0), x], axis=0)
    acc = x * w_ref[GDN_CONV - 1:GDN_CONV, :]
    for j in range(GDN_CONV - 1):
        acc = acc + pltpu.roll(xx, GDN_CONV - 1 - j, axis=0)[CONV_HALO:CONV_HALO + rows] * w_ref[j:j + 1, :]
    return acc * _sigmoid(acc)


def _l2_normalize(x):
    return x * lax.rsqrt(jnp.sum(x * x, axis=-1, keepdims=True) + NORM_EPS)


def _gdn_kernel(q_ref, k_ref, v_ref, z_ref, wq_ref, wk_ref, wv_ref, a_ref, b_ref, alog_ref, dtb_ref, onorm_ref,
                o_ref, *, seq):
    c = GDN_GROUP
    d = HEAD_DIM
    heads = v_ref.shape[1] // d
    ngrp = GDN_GROUPS_PER_STEP
    ii = lax.broadcasted_iota(jnp.int32, (c, c), 0)
    jj = lax.broadcasted_iota(jnp.int32, (c, c), 1)
    lower = ii >= jj
    strict = ii > jj
    eye_b = ii == jj
    eye = eye_b.astype(F32)
    ij_xor = ii ^ jj
    scale = d ** -0.5

    def step(it, states):
        rows, qs, ks, vs, kks, qks = [], [], [], [], [], []
        for gg in range(ngrp):
            r0 = pl.multiple_of((it * ngrp + gg) * c, c)
            q = _l2_normalize(_conv_silu(q_ref, wq_ref, r0, c)) * scale
            k = _l2_normalize(_conv_silu(k_ref, wk_ref, r0, c))
            k16 = k.astype(BF16)
            rows.append(r0)
            qs.append(q)
            ks.append(k)
            vs.append(_conv_silu(v_ref, wv_ref, r0, c))
            kks.append(_dot_nt(k16, k16))
            qks.append(_dot_nt(q.astype(BF16), k16))
        prob = []
        for gg in range(ngrp):
            for h in range(heads):
                a_row = a_ref[h:h + 1, pl.ds(rows[gg], c)]
                b_row = b_ref[h:h + 1, pl.ds(rows[gg], c)]
                x = a_row + dtb_ref[h:h + 1, :]
                softplus = jnp.maximum(x, 0.0) + jnp.log1p(jnp.exp(-jnp.abs(x)))
                g_row = -jnp.exp(alog_ref[h:h + 1, :]) * softplus
                beta_row = _sigmoid(b_row)
                gc_col = jnp.sum(jnp.where(lower, g_row, 0.0), axis=1, keepdims=True)
                gc_row = jnp.sum(jnp.where(eye_b, gc_col, 0.0), axis=0, keepdims=True)
                beta_col = jnp.sum(jnp.where(eye_b, beta_row, 0.0), axis=1, keepdims=True)
                decay = jnp.where(lower, jnp.exp(jnp.where(lower, gc_col - gc_row, 0.0)), 0.0)
                lmat = jnp.where(strict, beta_col * kks[gg] * decay, 0.0)
                prob.append(dict(gg=gg, h=h, gc_col=gc_col, beta_col=beta_col, decay=decay, lmat=lmat))
        tinv = _unit_lower_inverses([p["lmat"] for p in prob], eye, ij_xor)
        for p, t in zip(prob, tinv):
            gg, h, gc_col = p["gg"], p["h"], p["gc_col"]
            v = vs[gg][:, h * d:(h + 1) * d]
            rhs = p["beta_col"] * jnp.concatenate([v, ks[gg] * jnp.exp(gc_col)], axis=1)
            uw = _dot(t.astype(BF16), rhs.astype(BF16))
            p["u"] = uw[:, :d]
            p["w16"] = uw[:, d:].astype(BF16)
            p["attn16"] = jnp.where(lower, qks[gg] * p["decay"], 0.0).astype(BF16)
            p["qg16"] = (qs[gg] * jnp.exp(gc_col)).astype(BF16)
            g_last = gc_col[c - 1:c, :]
            p["kdec16"] = (ks[gg] * jnp.exp(g_last - gc_col)).astype(BF16)
            p["carry"] = jnp.exp(g_last)
        states = list(states)
        for p in prob:
            gg, h = p["gg"], p["h"]
            s = states[h]
            s16 = s.astype(BF16)
            v_new = p["u"] - _dot(p["w16"], s16)
            vn16 = v_new.astype(BF16)
            o = _dot(p["qg16"], s16) + _dot(p["attn16"], vn16)
            states[h] = s * p["carry"] + _dot_tn(p["kdec16"], vn16)
            ms = jnp.mean(o * o, axis=-1, keepdims=True)
            z = z_ref[pl.ds(rows[gg], c), h * d:(h + 1) * d]
            y = o * lax.rsqrt(ms + NORM_EPS) * onorm_ref[...] * (z * _sigmoid(z))
            o_ref[pl.ds(rows[gg], c), h * d:(h + 1) * d] = y.astype(o_ref.dtype)
        return tuple(states)

    init = tuple(jnp.zeros((d, d), F32) for _ in range(heads))
    lax.fori_loop(0, seq // (c * ngrp), step, init)


def gdn_core(proj, conv_w, a_t, b_t, alog, dtb, o_norm, batch, seq):
    d = HEAD_DIM
    rep = GDN_V_HEADS // GDN_QK_HEADS
    m = proj.shape[0]
    kb = GDN_QK_HEADS
    vb = 2 * GDN_QK_HEADS // rep
    zb = vb + GDN_V_HEADS // rep
    assert seq % (GDN_GROUP * GDN_GROUPS_PER_STEP) == 0
    return pl.pallas_call(
        functools.partial(_gdn_kernel, seq=seq),
        grid=(batch, GDN_QK_HEADS),
        in_specs=[pl.BlockSpec((seq, d), lambda b, j: (b, j)),
                  pl.BlockSpec((seq, d), lambda b, j: (b, kb + j)),
                  pl.BlockSpec((seq, rep * d), lambda b, j: (b, vb + j)),
                  pl.BlockSpec((seq, rep * d), lambda b, j: (b, zb + j)),
                  pl.BlockSpec((GDN_CONV, d), lambda b, j: (0, j)),
                  pl.BlockSpec((GDN_CONV, d), lambda b, j: (0, kb + j)),
                  pl.BlockSpec((GDN_CONV, rep * d), lambda b, j: (0, vb + j)),
                  pl.BlockSpec((None, None, rep, seq), lambda b, j: (b, j, 0, 0)),
                  pl.BlockSpec((None, None, rep, seq), lambda b, j: (b, j, 0, 0)),
                  pl.BlockSpec((None, rep, GDN_GROUP), lambda b, j: (j, 0, 0)),
                  pl.BlockSpec((None, rep, GDN_GROUP), lambda b, j: (j, 0, 0)),
                  pl.BlockSpec((1, d), lambda b, j: (0, 0))],
        out_specs=pl.BlockSpec((seq, rep * d), lambda b, j: (b, j)),
        out_shape=jax.ShapeDtypeStruct((m, GDN_V_HEADS * d), BF16),
        compiler_params=_params("parallel", "parallel"),
        name="gdn_core",
    )(proj, proj, proj, proj, conv_w, conv_w, conv_w, a_t, b_t, alog, dtb, o_norm.reshape(1, d))


def _norm_rope(x, gain, cos, sin):
    ms = jnp.mean(x * x, axis=-1, keepdims=True)
    y = x * lax.rsqrt(ms + NORM_EPS) * gain
    return y * cos + pltpu.roll(y, HEAD_DIM // 2, axis=1) * sin


def _moba_kernel(q_ref, k_ref, v_ref, qgain_ref, kgain_ref, cos_ref, sin_ref, o_ref,
                 kpad_ref, vt_ref, kmean_ref, sel_ref, *, nblk):
    blk = MOBA_BLOCK
    d = HEAD_DIM
    cb = MOBA_CHUNK
    pad = cb - 1
    heads = q_ref.shape[1] // d
    scale = d ** -0.5 * LOG2_E
    sel0 = SUBLANES

    for h in range(heads):
        kpad_ref[h, 0:pad * blk, :] = jnp.zeros((pad * blk, d), BF16)
        vt_ref[h, 0:pad] = jnp.zeros((pad, d, blk), BF16)
        sel_ref[h, 0:sel0, :] = jnp.zeros((sel0, blk), F32)
        for n in range(nblk):
            kb = _norm_rope(k_ref[n * blk:(n + 1) * blk, h * d:(h + 1) * d], kgain_ref[...],
                            cos_ref[n * blk:(n + 1) * blk, :], sin_ref[n * blk:(n + 1) * blk, :])
            kmean_ref[h, n:n + 1, :] = jnp.mean(kb, axis=0, keepdims=True)
            kpad_ref[h, (pad + n) * blk:(pad + n + 1) * blk, :] = kb.astype(BF16)
            vt_ref[h, pad + n] = v_ref[n * blk:(n + 1) * blk, h * d:(h + 1) * d].T.astype(BF16)

    nidx = lax.broadcasted_iota(jnp.int32, (nblk, blk), 0)
    kpos = lax.broadcasted_iota(jnp.int32, (blk, blk), 0)
    qpos = lax.broadcasted_iota(jnp.int32, (blk, blk), 1)
    causal = kpos <= qpos

    def chunk_scores(h, q16, first_blk, own):
        k0 = pl.multiple_of((first_blk + pad) * blk, blk)
        st = _dot_nt(kpad_ref[h, pl.ds(k0, cb * blk), :], q16)
        parts = []
        for u in range(cb):
            su = st[u * blk:(u + 1) * blk, :]
            if own and u == cb - 1:
                keep = causal
            else:
                keep = sel_ref[h, pl.ds(sel0 + first_blk + u, 1), :] > 0.0
            parts.append(jnp.where(keep, su, -jnp.inf))
        return parts

    def chunk_pv(h, first_blk, p_parts):
        acc = None
        for u in range(cb):
            term = _dot(vt_ref[h, first_blk + pad + u], p_parts[u].astype(BF16))
            acc = term if acc is None else acc + term
        return acc

    def qblock(qb, carry):
        r0 = pl.multiple_of(qb * blk, blk)
        hs = range(heads)
        cos_q, sin_q = cos_ref[pl.ds(r0, blk), :], sin_ref[pl.ds(r0, blk), :]
        qs = [_norm_rope(q_ref[pl.ds(r0, blk), h * d:(h + 1) * d], qgain_ref[...], cos_q, sin_q) for h in hs]
        q16s = [(q * scale).astype(BF16) for q in qs]
        gates = [lax.dot_general(kmean_ref[h], qs[h], (((1,), (1,)), ((), ())),
                                 precision=lax.Precision.HIGHEST, preferred_element_type=F32) for h in hs]
        for h in hs:
            rank = jnp.zeros((nblk, blk), jnp.int32)
            for m in range(nblk):
                gm = gates[h][m:m + 1, :]
                beats = (gm > gates[h]) | ((gm == gates[h]) & (m < nidx))
                rank = rank + jnp.where(beats & (m < qb), 1, 0)
            sel_ref[h, sel0:sel0 + nblk, :] = jnp.where((nidx < qb) & (rank < MOBA_TOP_K), 1.0, 0.0)
        parts0 = [chunk_scores(h, q16s[h], qb - pad, own=True) for h in hs]
        m0 = [functools.reduce(jnp.maximum, [jnp.max(s, axis=0, keepdims=True) for s in parts0[h]]) for h in hs]
        p0 = [[jnp.exp2(s - m0[h]) for s in parts0[h]] for h in hs]
        l0 = [functools.reduce(jnp.add, [jnp.sum(p, axis=0, keepdims=True) for p in p0[h]]) for h in hs]
        stats = [(m0[h], l0[h], chunk_pv(h, qb - pad, p0[h])) for h in hs]

        def older(t, st):
            first_blk = qb - pad - cb * t
            hs = range(heads)
            parts = [chunk_scores(h, q16s[h], first_blk, own=False) for h in hs]
            m_new = [functools.reduce(jnp.maximum, [st[h][0]] + [jnp.max(s, axis=0, keepdims=True) for s in parts[h]])
                     for h in hs]
            alpha = [jnp.exp2(st[h][0] - m_new[h]) for h in hs]
            p_parts = [[jnp.exp2(s - m_new[h]) for s in parts[h]] for h in hs]
            l_new = [alpha[h] * st[h][1] + functools.reduce(jnp.add, [jnp.sum(p, axis=0, keepdims=True) for p in p_parts[h]])
                     for h in hs]
            acc = [alpha[h] * st[h][2] + chunk_pv(h, first_blk, p_parts[h]) for h in hs]
            return tuple((m_new[h], l_new[h], acc[h]) for h in hs)

        final = lax.fori_loop(1, qb // cb + 1, older, tuple(stats))
        for h in range(heads):
            _, l_f, acc_f = final[h]
            o_ref[pl.ds(r0, blk), h * d:(h + 1) * d] = (acc_f / l_f).T.astype(o_ref.dtype)
        return carry

    lax.fori_loop(0, nblk, qblock, 0)


def moba_core(proj, q_off, k_off, v_off, q_gain, k_gain, cos_t, sin_t, batch, seq):
    d = HEAD_DIM
    hp = MOBA_HEADS_PER_STEP
    m = proj.shape[0]
    nblk = seq // MOBA_BLOCK
    pad = MOBA_CHUNK - 1
    qb, kb, vb = q_off // (hp * d), k_off // (hp * d), v_off // (hp * d)
    return pl.pallas_call(
        functools.partial(_moba_kernel, nblk=nblk),
        grid=(batch, MOBA_HEADS // hp),
        in_specs=[pl.BlockSpec((seq, hp * d), lambda b, h: (b, qb + h)),
                  pl.BlockSpec((seq, hp * d), lambda b, h: (b, kb + h)),
                  pl.BlockSpec((seq, hp * d), lambda b, h: (b, vb + h)),
                  pl.BlockSpec((1, d), lambda b, h: (0, 0)),
                  pl.BlockSpec((1, d), lambda b, h: (0, 0)),
                  pl.BlockSpec((seq, d), lambda b, h: (0, 0)),
                  pl.BlockSpec((seq, d), lambda b, h: (0, 0))],
        out_specs=pl.BlockSpec((seq, hp * d), lambda b, h: (b, h)),
        out_shape=jax.ShapeDtypeStruct((m, MOBA_HEADS * d), BF16),
        scratch_shapes=[pltpu.VMEM((hp, (nblk + pad) * MOBA_BLOCK, d), BF16),
                        pltpu.VMEM((hp, nblk + pad, d, MOBA_BLOCK), BF16),
                        pltpu.VMEM((hp, nblk, d), F32),
                        pltpu.VMEM((hp, SUBLANES + nblk, MOBA_BLOCK), F32)],
        compiler_params=_params("parallel", "parallel"),
        name="moba_core",
    )(proj, proj, proj, q_gain.reshape(1, d), k_gain.reshape(1, d), cos_t, sin_t)


def _swiglu_half_step(x, norm, w_gate, w_up, w_down):
    h = rmsnorm_bf16(x, norm)
    act = ffn_up(h, w_gate, w_up)
    return matmul_residual(act, w_down.astype(BF16), x, 0.5, tm=512, tn=512)


def _rope_tables(seq):
    half = HEAD_DIM // 2
    inv_freq = ROPE_THETA ** (-jnp.arange(half, dtype=F32) / half)
    ang = jnp.arange(seq).astype(F32)[:, None] * inv_freq[None, :]
    cos, sin = jnp.cos(ang), jnp.sin(ang)
    return jnp.concatenate([cos, cos], axis=1), jnp.concatenate([-sin, sin], axis=1)


def _layer(x, ffn1_norm, ffn1_w_gate, ffn1_w_up, ffn1_w_down, mix_norm, w_in, gdn_conv_w, gdn_A_log,
           gdn_dt_bias, gdn_o_norm, moba_q_norm, moba_k_norm, w_branch_gdn, w_branch_moba, w_out,
           ffn2_norm, ffn2_w_gate, ffn2_w_up, ffn2_w_down, batch, seq):
    d = HEAD_DIM
    qkw = GDN_QK_HEADS * d
    vw = GDN_V_HEADS * d
    mw = MOBA_HEADS * d
    dm = x.shape[1]

    x = _swiglu_half_step(x, ffn1_norm, ffn1_w_gate, ffn1_w_up, ffn1_w_down)

    ab_off = 2 * qkw + 2 * vw
    w_ab = jnp.pad(w_in[:, ab_off:ab_off + 2 * GDN_V_HEADS], ((0, 0), (0, d - 2 * GDN_V_HEADS))).astype(BF16)
    w_rest = w_in[:, ab_off + 2 * GDN_V_HEADS:].astype(BF16)
    mq_off, mk_off, mv_off, gg_off, gm_off = 0, mw, 2 * mw, 3 * mw, 3 * mw + dm

    h = rmsnorm_bf16(x, mix_norm)
    proj_g = matmul(h, w_in, ab_off, F32)
    ab = matmul(h, w_ab, d, F32)
    proj_m = matmul(h, w_rest, w_rest.shape[1], F32)

    rep = GDN_V_HEADS // GDN_QK_HEADS
    to_heads = lambda t: t.reshape(batch, seq, GDN_QK_HEADS, rep).transpose(0, 2, 3, 1)
    a_t = to_heads(ab[:, :GDN_V_HEADS])
    b_t = to_heads(ab[:, GDN_V_HEADS:2 * GDN_V_HEADS])
    per_head = lambda t: jnp.broadcast_to(t.astype(F32).reshape(GDN_QK_HEADS, rep, 1), (GDN_QK_HEADS, rep, GDN_GROUP))
    o_gdn = gdn_core(proj_g, gdn_conv_w, a_t, b_t, per_head(gdn_A_log), per_head(gdn_dt_bias), gdn_o_norm, batch, seq)

    cos_t, sin_t = _rope_tables(seq)
    o_moba = moba_core(proj_m, mq_off, mk_off, mv_off, moba_q_norm, moba_k_norm, cos_t, sin_t, batch, seq)

    y = merge_branches(o_gdn, o_moba, w_branch_gdn.astype(BF16), w_branch_moba.astype(BF16), proj_m, gg_off, gm_off)
    x = matmul_residual(y, w_out.astype(BF16), x, 1.0, tm=1024, tn=512)

    return _swiglu_half_step(x, ffn2_norm, ffn2_w_gate, ffn2_w_up, ffn2_w_down)


def kernel(x, ffn1_norm, ffn1_w_gate, ffn1_w_up, ffn1_w_down, mix_norm, w_in, gdn_conv_w, gdn_A_log, gdn_dt_bias,
           gdn_o_norm, moba_q_norm, moba_k_norm, w_branch_gdn, w_branch_moba, w_out, ffn2_norm, ffn2_w_gate,
           ffn2_w_up, ffn2_w_down):
    batch, seq, dm = x.shape
    assert ffn1_norm.shape[0] == 1, "single-layer stack"
    out = _layer(x.reshape(batch * seq, dm), ffn1_norm[0], ffn1_w_gate[0], ffn1_w_up[0], ffn1_w_down[0],
                 mix_norm[0], w_in[0], gdn_conv_w[0], gdn_A_log[0], gdn_dt_bias[0], gdn_o_norm[0],
                 moba_q_norm[0], moba_k_norm[0], w_branch_gdn[0], w_branch_moba[0], w_out[0],
                 ffn2_norm[0], ffn2_w_gate[0], ffn2_w_up[0], ffn2_w_down[0], batch, seq)
    return out.reshape(batch, seq, dm)
```

```python
import functools

import jax
import jax.numpy as jnp
from jax import lax
from jax.experimental import pallas as pl
from jax.experimental.pallas import tpu as pltpu

F32 = jnp.float32
BF16 = jnp.bfloat16

NORM_EPS = 1e-6
HEAD_DIM = 128
GDN_QK_HEADS = 16
GDN_V_HEADS = 32
GDN_CONV = 4
GDN_GROUP = 256
GDN_GROUPS_PER_STEP = 4
MOBA_HEADS = 16
MOBA_HEADS_PER_STEP = 2
MOBA_BLOCK = 256
MOBA_TOP_K = 3
MOBA_CHUNK = 4
ROPE_THETA = 10000.0
LOG2_E = 1.4426950408889634
CONV_HALO = 8
SUBLANES = 8

VMEM_LIMIT = 56 * 1024 * 1024


def _params(*sem):
    return pltpu.CompilerParams(dimension_semantics=sem, vmem_limit_bytes=VMEM_LIMIT)


def _sigmoid(x):
    return 1.0 / (1.0 + jnp.exp(-x))


def _dot(a, b):
    return jnp.dot(a, b, preferred_element_type=F32)


def _dot_nt(a, b):
    return lax.dot_general(a, b, (((1,), (1,)), ((), ())), preferred_element_type=F32)


def _dot_tn(a, b):
    return lax.dot_general(a, b, (((0,), (0,)), ((), ())), preferred_element_type=F32)


def _rmsnorm_kernel(x_ref, g_ref, o_ref):
    x = x_ref[...]
    ms = jnp.mean(x * x, axis=-1, keepdims=True)
    o_ref[...] = (x * lax.rsqrt(ms + NORM_EPS) * g_ref[...]).astype(o_ref.dtype)


def rmsnorm_bf16(x, gain, tm=256):
    m, d = x.shape
    return pl.pallas_call(
        _rmsnorm_kernel,
        grid=(m // tm,),
        in_specs=[pl.BlockSpec((tm, d), lambda i: (i, 0)),
                  pl.BlockSpec((1, d), lambda i: (0, 0))],
        out_specs=pl.BlockSpec((tm, d), lambda i: (i, 0)),
        out_shape=jax.ShapeDtypeStruct((m, d), BF16),
        compiler_params=_params("parallel"),
        name="rmsnorm",
    )(x, gain.reshape(1, d))


def _ffn_up_kernel(h_ref, wg_ref, wu_ref, o_ref):
    h = h_ref[...]
    g = _dot(h, wg_ref[...].astype(BF16))
    u = _dot(h, wu_ref[...].astype(BF16))
    o_ref[...] = (g * _sigmoid(g) * u).astype(o_ref.dtype)


def ffn_up(h, wg, wu, tm=1024, tn=256):
    m, k = h.shape
    n = wg.shape[1]
    return pl.pallas_call(
        _ffn_up_kernel,
        grid=(m // tm, n // tn),
        in_specs=[pl.BlockSpec((tm, k), lambda i, j: (i, 0)),
                  pl.BlockSpec((k, tn), lambda i, j: (0, j)),
                  pl.BlockSpec((k, tn), lambda i, j: (0, j))],
        out_specs=pl.BlockSpec((tm, tn), lambda i, j: (i, j)),
        out_shape=jax.ShapeDtypeStruct((m, n), BF16),
        compiler_params=_params("parallel", "arbitrary"),
        name="ffn_up",
    )(h, wg, wu)


def _mm_kernel(a_ref, b_ref, o_ref):
    o_ref[...] = _dot(a_ref[...], b_ref[...].astype(BF16)).astype(o_ref.dtype)


def matmul(a, b, n, out_dtype, tm=1024, tn=512):
    m, k = a.shape
    tn = min(tn, n)
    return pl.pallas_call(
        _mm_kernel,
        grid=(m // tm, n // tn),
        in_specs=[pl.BlockSpec((tm, k), lambda i, j: (i, 0)),
                  pl.BlockSpec((k, tn), lambda i, j: (0, j))],
        out_specs=pl.BlockSpec((tm, tn), lambda i, j: (i, j)),
        out_shape=jax.ShapeDtypeStruct((m, n), out_dtype),
        compiler_params=_params("parallel", "arbitrary"),
        name="matmul",
    )(a, b)


def _mm_residual_kernel(a_ref, b_ref, r_ref, o_ref, *, scale):
    o_ref[...] = r_ref[...] + scale * _dot(a_ref[...], b_ref[...])


def matmul_residual(a, b, res, scale, tm, tn):
    m, k = a.shape
    n = b.shape[1]
    return pl.pallas_call(
        functools.partial(_mm_residual_kernel, scale=scale),
        grid=(m // tm, n // tn),
        in_specs=[pl.BlockSpec((tm, k), lambda i, j: (i, 0)),
                  pl.BlockSpec((k, tn), lambda i, j: (0, j)),
                  pl.BlockSpec((tm, tn), lambda i, j: (i, j))],
        out_specs=pl.BlockSpec((tm, tn), lambda i, j: (i, j)),
        out_shape=jax.ShapeDtypeStruct((m, n), F32),
        compiler_params=_params("parallel", "arbitrary"),
        name="matmul_residual",
    )(a, b, res)


def _merge_kernel(og_ref, om_ref, wg_ref, wm_ref, gg_ref, gm_ref, o_ref):
    yg = _dot(og_ref[...], wg_ref[...])
    ym = _dot(om_ref[...], wm_ref[...])
    o_ref[...] = (_sigmoid(gg_ref[...]) * yg + _sigmoid(gm_ref[...]) * ym).astype(o_ref.dtype)


def merge_branches(o_gdn, o_moba, wg, wm, proj, gg_off, gm_off, tm=1024, tn=512):
    m, kg = o_gdn.shape
    km = o_moba.shape[1]
    n = wg.shape[1]
    ggb, gmb = gg_off // tn, gm_off // tn
    return pl.pallas_call(
        _merge_kernel,
        grid=(m // tm, n // tn),
        in_specs=[pl.BlockSpec((tm, kg), lambda i, j: (i, 0)),
                  pl.BlockSpec((tm, km), lambda i, j: (i, 0)),
                  pl.BlockSpec((kg, tn), lambda i, j: (0, j)),
                  pl.BlockSpec((km, tn), lambda i, j: (0, j)),
                  pl.BlockSpec((tm, tn), lambda i, j: (i, ggb + j)),
                  pl.BlockSpec((tm, tn), lambda i, j: (i, gmb + j))],
        out_specs=pl.BlockSpec((tm, tn), lambda i, j: (i, j)),
        out_shape=jax.ShapeDtypeStruct((m, n), BF16),
        compiler_params=_params("parallel", "arbitrary"),
        name="merge_branches",
    )(o_gdn, o_moba, wg, wm, proj, proj)


def _split_bf16(a):
    hi = a.astype(BF16)
    lo = (a - hi.astype(F32)).astype(BF16)
    return hi, lo


def _mm3(a_split, b_split):
    ah, al = a_split
    bh, bl = b_split
    lhs = jnp.concatenate([ah, ah, al], axis=1)
    rhs = jnp.concatenate([bh, bl, bh], axis=0)
    return _dot(lhs, rhs)


def _odd_blocks(x, n):
    return jnp.concatenate([x[(2 * i + 1) * n:(2 * i + 2) * n] for i in range(x.shape[0] // (2 * n))], axis=0)


def _zero_even_blocks(y, n):
    zeros = jnp.zeros((n, y.shape[1]), y.dtype)
    pieces = []
    for i in range(y.shape[0] // n):
        pieces += [zeros, y[i * n:(i + 1) * n]]
    return jnp.concatenate(pieces, axis=0)


def _sub_from_odd_blocks(x, y, n):
    pieces = []
    for i in range(x.shape[0] // (2 * n)):
        pieces += [x[2 * i * n:(2 * i + 1) * n], x[(2 * i + 1) * n:(2 * i + 2) * n] - y[i * n:(i + 1) * n]]
    return jnp.concatenate(pieces, axis=0)


def _unit_lower_inverses(lmats, eye, ij_xor):
    size = eye.shape[0]
    dinvs = [eye - jnp.where(ij_xor == 1, lm, 0.0) for lm in lmats]
    level = 1
    while (2 << level) <= size:
        n = 1 << level
        mask = (ij_xor >> level) == 1
        d16 = [dv.astype(BF16) for dv in dinvs]
        if n < SUBLANES:
            cd = [_dot(jnp.where(mask, lm, 0.0).astype(BF16), dh) for lm, dh in zip(lmats, d16)]
            dinvs = [dv - _dot(dh, m.astype(BF16)) for dv, dh, m in zip(dinvs, d16, cd)]
        else:
            mask_odd = (_odd_blocks(ij_xor, n) >> level) == 1
            cd = [_dot(jnp.where(mask_odd, _odd_blocks(lm, n), 0.0).astype(BF16), dh) for lm, dh in zip(lmats, d16)]
            upd = [_dot(_odd_blocks(dv, n).astype(BF16), _zero_even_blocks(m, n).astype(BF16))
                   for dv, m in zip(dinvs, cd)]
            dinvs = [_sub_from_odd_blocks(dv, u, n) for dv, u in zip(dinvs, upd)]
        level += 1
    resid = [eye - dv - _mm3(_split_bf16(lm), _split_bf16(dv)) for lm, dv in zip(lmats, dinvs)]
    return [dv + _dot(dv.astype(BF16), r.astype(BF16)) for dv, r in zip(dinvs, resid)]


def _conv_silu(x_ref, w_ref, r0, rows):
    x = x_ref[pl.ds(r0, rows), :]
    prev = x_ref[pl.ds(pl.multiple_of(jnp.maximum(r0 - CONV_HALO, 0), CONV_HALO), CONV_HALO), :]
    xx = jnp.concatenate([jnp.where(r0 > 0, prev, 0.0), x], axis=0)
    acc = x * w_ref[GDN_CONV - 1:GDN_CONV, :]
    for j in range(GDN_CONV - 1):
        acc = acc + pltpu.roll(xx, GDN_CONV - 1 - j, axis=0)[CONV_HALO:CONV_HALO + rows] * w_ref[j:j + 1, :]
    return acc * _sigmoid(acc)


def _l2_normalize(x):
    return x * lax.rsqrt(jnp.sum(x * x, axis=-1, keepdims=True) + NORM_EPS)


def _gdn_kernel(q_ref, k_ref, v_ref, z_ref, wq_ref, wk_ref, wv_ref, a_ref, b_ref, alog_ref, dtb_ref, onorm_ref,
                o_ref, *, seq):
    c = GDN_GROUP
    d = HEAD_DIM
    heads = v_ref.shape[1] // d
    ngrp = GDN_GROUPS_PER_STEP
    ii = lax.broadcasted_iota(jnp.int32, (c, c), 0)
    jj = lax.broadcasted_iota(jnp.int32, (c, c), 1)
    lower = ii >= jj
    strict = ii > jj
    eye_b = ii == jj
    eye = eye_b.astype(F32)
    ij_xor = ii ^ jj
    scale = d ** -0.5

    def step(it, states):
        rows, qs, ks, vs, kks, qks = [], [], [], [], [], []
        for gg in range(ngrp):
            r0 = pl.multiple_of((it * ngrp + gg) * c, c)
            q = _l2_normalize(_conv_silu(q_ref, wq_ref, r0, c)) * scale
            k = _l2_normalize(_conv_silu(k_ref, wk_ref, r0, c))
            k16 = k.astype(BF16)
            rows.append(r0)
            qs.append(q)
            ks.append(k)
            vs.append(_conv_silu(v_ref, wv_ref, r0, c))
            kks.append(_dot_nt(k16, k16))
            qks.append(_dot_nt(q.astype(BF16), k16))
        prob = []
        for gg in range(ngrp):
            for h in range(heads):
                a_row = a_ref[h:h + 1, pl.ds(rows[gg], c)]
                b_row = b_ref[h:h + 1, pl.ds(rows[gg], c)]
                x = a_row + dtb_ref[h:h + 1, :]
                softplus = jnp.maximum(x, 0.0) + jnp.log1p(jnp.exp(-jnp.abs(x)))
                g_row = -jnp.exp(alog_ref[h:h + 1, :]) * softplus
                beta_row = _sigmoid(b_row)
                gc_col = jnp.sum(jnp.where(lower, g_row, 0.0), axis=1, keepdims=True)
                gc_row = jnp.sum(jnp.where(eye_b, gc_col, 0.0), axis=0, keepdims=True)
                beta_col = jnp.sum(jnp.where(eye_b, beta_row, 0.0), axis=1, keepdims=True)
                decay = jnp.where(lower, jnp.exp(jnp.where(lower, gc_col - gc_row, 0.0)), 0.0)
                lmat = jnp.where(strict, beta_col * kks[gg] * decay, 0.0)
                prob.append(dict(gg=gg, h=h, gc_col=gc_col, beta_col=beta_col, decay=decay, lmat=lmat))
        tinv = _unit_lower_inverses([p["lmat"] for p in prob], eye, ij_xor)
        for p, t in zip(prob, tinv):
            gg, h, gc_col = p["gg"], p["h"], p["gc_col"]
            v = vs[gg][:, h * d:(h + 1) * d]
            rhs = p["beta_col"] * jnp.concatenate([v, ks[gg] * jnp.exp(gc_col)], axis=1)
            uw = _dot(t.astype(BF16), rhs.astype(BF16))
            p["u"] = uw[:, :d]
            p["w16"] = uw[:, d:].astype(BF16)
            p["attn16"] = jnp.where(lower, qks[gg] * p["decay"], 0.0).astype(BF16)
            p["qg16"] = (qs[gg] * jnp.exp(gc_col)).astype(BF16)
            g_last = gc_col[c - 1:c, :]
            p["kdec16"] = (ks[gg] * jnp.exp(g_last - gc_col)).astype(BF16)
            p["carry"] = jnp.exp(g_last)
        states = list(states)
        for p in prob:
            gg, h = p["gg"], p["h"]
            s = states[h]
            s16 = s.astype(BF16)
            v_new = p["u"] - _dot(p["w16"], s16)
            vn16 = v_new.astype(BF16)
            o = _dot(p["qg16"], s16) + _dot(p["attn16"], vn16)
            states[h] = s * p["carry"] + _dot_tn(p["kdec16"], vn16)
            ms = jnp.mean(o * o, axis=-1, keepdims=True)
            z = z_ref[pl.ds(rows[gg], c), h * d:(h + 1) * d]
            y = o * lax.rsqrt(ms + NORM_EPS) * onorm_ref[...] * (z * _sigmoid(z))
            o_ref[pl.ds(rows[gg], c), h * d:(h + 1) * d] = y.astype(o_ref.dtype)
        return tuple(states)

    init = tuple(jnp.zeros((d, d), F32) for _ in range(heads))
    lax.fori_loop(0, seq // (c * ngrp), step, init)


def gdn_core(proj, conv_w, a_t, b_t, alog, dtb, o_norm, batch, seq):
    d = HEAD_DIM
    rep = GDN_V_HEADS // GDN_QK_HEADS
    m = proj.shape[0]
    kb = GDN_QK_HEADS
    vb = 2 * GDN_QK_HEADS // rep
    zb = vb + GDN_V_HEADS // rep
    assert seq % (GDN_GROUP * GDN_GROUPS_PER_STEP) == 0
    return pl.pallas_call(
        functools.partial(_gdn_kernel, seq=seq),
        grid=(batch, GDN_QK_HEADS),
        in_specs=[pl.BlockSpec((seq, d), lambda b, j: (b, j)),
                  pl.BlockSpec((seq, d), lambda b, j: (b, kb + j)),
                  pl.BlockSpec((seq, rep * d), lambda b, j: (b, vb + j)),
                  pl.BlockSpec((seq, rep * d), lambda b, j: (b, zb + j)),
                  pl.BlockSpec((GDN_CONV, d), lambda b, j: (0, j)),
                  pl.BlockSpec((GDN_CONV, d), lambda b, j: (0, kb + j)),
                  pl.BlockSpec((GDN_CONV, rep * d), lambda b, j: (0, vb + j)),
                  pl.BlockSpec((None, None, rep, seq), lambda b, j: (b, j, 0, 0)),
                  pl.BlockSpec((None, None, rep, seq), lambda b, j: (b, j, 0, 0)),
                  pl.BlockSpec((None, rep, GDN_GROUP), lambda b, j: (j, 0, 0)),
                  pl.BlockSpec((None, rep, GDN_GROUP), lambda b, j: (j, 0, 0)),
                  pl.BlockSpec((1, d), lambda b, j: (0, 0))],
        out_specs=pl.BlockSpec((seq, rep * d), lambda b, j: (b, j)),
        out_shape=jax.ShapeDtypeStruct((m, GDN_V_HEADS * d), BF16),
        compiler_params=_params("parallel", "parallel"),
        name="gdn_core",
    )(proj, proj, proj, proj, conv_w, conv_w, conv_w, a_t, b_t, alog, dtb, o_norm.reshape(1, d))


def _norm_rope(x, gain, cos, sin):
    ms = jnp.mean(x * x, axis=-1, keepdims=True)
    y = x * lax.rsqrt(ms + NORM_EPS) * gain
    return y * cos + pltpu.roll(y, HEAD_DIM // 2, axis=1) * sin


def _moba_kernel(q_ref, k_ref, v_ref, qgain_ref, kgain_ref, cos_ref, sin_ref, o_ref,
                 kpad_ref, vt_ref, kmean_ref, sel_ref, *, nblk):
    blk = MOBA_BLOCK
    d = HEAD_DIM
    cb = MOBA_CHUNK
    pad = cb - 1
    heads = q_ref.shape[1] // d
    scale = d ** -0.5 * LOG2_E
    sel0 = SUBLANES

    for h in range(heads):
        kpad_ref[h, 0:pad * blk, :] = jnp.zeros((pad * blk, d), BF16)
        vt_ref[h, 0:pad] = jnp.zeros((pad, d, blk), BF16)
        sel_ref[h, 0:sel0, :] = jnp.zeros((sel0, blk), F32)
        for n in range(nblk):
            kb = _norm_rope(k_ref[n * blk:(n + 1) * blk, h * d:(h + 1) * d], kgain_ref[...],
                            cos_ref[n * blk:(n + 1) * blk, :], sin_ref[n * blk:(n + 1) * blk, :])
            kmean_ref[h, n:n + 1, :] = jnp.mean(kb, axis=0, keepdims=True)
            kpad_ref[h, (pad + n) * blk:(pad + n + 1) * blk, :] = kb.astype(BF16)
            vt_ref[h, pad + n] = v_ref[n * blk:(n + 1) * blk, h * d:(h + 1) * d].T.astype(BF16)

    nidx = lax.broadcasted_iota(jnp.int32, (nblk, blk), 0)
    kpos = lax.broadcasted_iota(jnp.int32, (blk, blk), 0)
    qpos = lax.broadcasted_iota(jnp.int32, (blk, blk), 1)
    causal = kpos <= qpos

    def chunk_scores(h, q16, first_blk, own):
        k0 = pl.multiple_of((first_blk + pad) * blk, blk)
        st = _dot_nt(kpad_ref[h, pl.ds(k0, cb * blk), :], q16)
        parts = []
        for u in range(cb):
            su = st[u * blk:(u + 1) * blk, :]
            if own and u == cb - 1:
                keep = causal
            else:
                keep = sel_ref[h, pl.ds(sel0 + first_blk + u, 1), :] > 0.0
            parts.append(jnp.where(keep, su, -jnp.inf))
        return parts

    def chunk_pv(h, first_blk, p_parts):
        acc = None
        for u in range(cb):
            term = _dot(vt_ref[h, first_blk + pad + u], p_parts[u].astype(BF16))
            acc = term if acc is None else acc + term
        return acc

    def qblock(qb, carry):
        r0 = pl.multiple_of(qb * blk, blk)
        hs = range(heads)
        cos_q, sin_q = cos_ref[pl.ds(r0, blk), :], sin_ref[pl.ds(r0, blk), :]
        qs = [_norm_rope(q_ref[pl.ds(r0, blk), h * d:(h + 1) * d], qgain_ref[...], cos_q, sin_q) for h in hs]
        q16s = [(q * scale).astype(BF16) for q in qs]
        gates = [lax.dot_general(kmean_ref[h], qs[h], (((1,), (1,)), ((), ())),
                                 precision=lax.Precision.HIGHEST, preferred_element_type=F32) for h in hs]
        for h in hs:
            rank = jnp.zeros((nblk, blk), jnp.int32)
            for m in range(nblk):
                gm = gates[h][m:m + 1, :]
                beats = (gm > gates[h]) | ((gm == gates[h]) & (m < nidx))
                rank = rank + jnp.where(beats & (m < qb), 1, 0)
            sel_ref[h, sel0:sel0 + nblk, :] = jnp.where((nidx < qb) & (rank < MOBA_TOP_K), 1.0, 0.0)
        parts0 = [chunk_scores(h, q16s[h], qb - pad, own=True) for h in hs]
        m0 = [functools.reduce(jnp.maximum, [jnp.max(s, axis=0, keepdims=True) for s in parts0[h]]) for h in hs]
        p0 = [[jnp.exp2(s - m0[h]) for s in parts0[h]] for h in hs]
        l0 = [functools.reduce(jnp.add, [jnp.sum(p, axis=0, keepdims=True) for p in p0[h]]) for h in hs]
        stats = [(m0[h], l0[h], chunk_pv(h, qb - pad, p0[h])) for h in hs]

        def older(t, st):
            first_blk = qb - pad - cb * t
            hs = range(heads)
            parts = [chunk_scores(h, q16s[h], first_blk, own=False) for h in hs]
            m_new = [functools.reduce(jnp.maximum, [st[h][0]] + [jnp.max(s, axis=0, keepdims=True) for s in parts[h]])
                     for h in hs]
            alpha = [jnp.exp2(st[h][0] - m_new[h]) for h in hs]
            p_parts = [[jnp.exp2(s - m_new[h]) for s in parts[h]] for h in hs]
            l_new = [alpha[h] * st[h][1] + functools.reduce(jnp.add, [jnp.sum(p, axis=0, keepdims=True) for p in p_parts[h]])
                     for h in hs]
            acc = [alpha[h] * st[h][2] + chunk_pv(h, first_blk, p_parts[h]) for h in hs]
            return tuple((m_new[h], l_new[h], acc[h]) for h in hs)

        final = lax.fori_loop(1, qb // cb + 1, older, tuple(stats))
        for h in range(heads):
            _, l_f, acc_f = final[h]
            o_ref[pl.ds(r0, blk), h * d:(h + 1) * d] = (acc_f / l_f).T.astype(o_ref.dtype)
        return carry

    lax.fori_loop(0, nblk, qblock, 0)


def moba_core(proj, q_off, k_off, v_off, q_gain, k_gain, cos_t, sin_t, batch, seq):
    d = HEAD_DIM
    hp = MOBA_HEADS_PER_STEP
    m = proj.shape[0]
    nblk = seq // MOBA_BLOCK
    pad = MOBA_CHUNK - 1
    qb, kb, vb = q_off // (hp * d), k_off // (hp * d), v_off // (hp * d)
    return pl.pallas_call(
        functools.partial(_moba_kernel, nblk=nblk),
        grid=(batch, MOBA_HEADS // hp),
        in_specs=[pl.BlockSpec((seq, hp * d), lambda b, h: (b, qb + h)),
                  pl.BlockSpec((seq, hp * d), lambda b, h: (b, kb + h)),
                  pl.BlockSpec((seq, hp * d), lambda b, h: (b, vb + h)),
                  pl.BlockSpec((1, d), lambda b, h: (0, 0)),
                  pl.BlockSpec((1, d), lambda b, h: (0, 0)),
                  pl.BlockSpec((seq, d), lambda b, h: (0, 0)),
                  pl.BlockSpec((seq, d), lambda b, h: (0, 0))],
        out_specs=pl.BlockSpec((seq, hp * d), lambda b, h: (b, h)),
        out_shape=jax.ShapeDtypeStruct((m, MOBA_HEADS * d), BF16),
        scratch_shapes=[pltpu.VMEM((hp, (nblk + pad) * MOBA_BLOCK, d), BF16),
                        pltpu.VMEM((hp, nblk + pad, d, MOBA_BLOCK), BF16),
                        pltpu.VMEM((hp, nblk, d), F32),
                        pltpu.VMEM((hp, SUBLANES + nblk, MOBA_BLOCK), F32)],
        compiler_params=_params("parallel", "parallel"),
        name="moba_core",
    )(proj, proj, proj, q_gain.reshape(1, d), k_gain.reshape(1, d), cos_t, sin_t)


def _swiglu_half_step(x, norm, w_gate, w_up, w_down):
    h = rmsnorm_bf16(x, norm)
    act = ffn_up(h, w_gate, w_up)
    return matmul_residual(act, w_down.astype(BF16), x, 0.5, tm=512, tn=512)


def _rope_tables(seq):
    half = HEAD_DIM // 2
    inv_freq = ROPE_THETA ** (-jnp.arange(half, dtype=F32) / half)
    ang = jnp.arange(seq).astype(F32)[:, None] * inv_freq[None, :]
    cos, sin = jnp.cos(ang), jnp.sin(ang)
    return jnp.concatenate([cos, cos], axis=1), jnp.concatenate([-sin, sin], axis=1)


def _layer(x, ffn1_norm, ffn1_w_gate, ffn1_w_up, ffn1_w_down, mix_norm, w_in, gdn_conv_w, gdn_A_log,
           gdn_dt_bias, gdn_o_norm, moba_q_norm, moba_k_norm, w_branch_gdn, w_branch_moba, w_out,
           ffn2_norm, ffn2_w_gate, ffn2_w_up, ffn2_w_down, batch, seq):
    d = HEAD_DIM
    qkw = GDN_QK_HEADS * d
    vw = GDN_V_HEADS * d
    mw = MOBA_HEADS * d
    dm = x.shape[1]

    x = _swiglu_half_step(x, ffn1_norm, ffn1_w_gate, ffn1_w_up, ffn1_w_down)

    ab_off = 2 * qkw + 2 * vw
    w_ab = jnp.pad(w_in[:, ab_off:ab_off + 2 * GDN_V_HEADS], ((0, 0), (0, d - 2 * GDN_V_HEADS))).astype(BF16)
    w_in16 = w_in.astype(BF16)
    w_rest = w_in16[:, ab_off + 2 * GDN_V_HEADS:]
    mq_off, mk_off, mv_off, gg_off, gm_off = 0, mw, 2 * mw, 3 * mw, 3 * mw + dm

    h = rmsnorm_bf16(x, mix_norm)
    proj_g = matmul(h, w_in16, ab_off, F32)
    ab = matmul(h, w_ab, d, F32)
    proj_m = matmul(h, w_rest, w_rest.shape[1], F32)

    rep = GDN_V_HEADS // GDN_QK_HEADS
    to_heads = lambda t: t.reshape(batch, seq, GDN_QK_HEADS, rep).transpose(0, 2, 3, 1)
    a_t = to_heads(ab[:, :GDN_V_HEADS])
    b_t = to_heads(ab[:, GDN_V_HEADS:2 * GDN_V_HEADS])
    per_head = lambda t: jnp.broadcast_to(t.astype(F32).reshape(GDN_QK_HEADS, rep, 1), (GDN_QK_HEADS, rep, GDN_GROUP))
    o_gdn = gdn_core(proj_g, gdn_conv_w, a_t, b_t, per_head(gdn_A_log), per_head(gdn_dt_bias), gdn_o_norm, batch, seq)

    cos_t, sin_t = _rope_tables(seq)
    o_moba = moba_core(proj_m, mq_off, mk_off, mv_off, moba_q_norm, moba_k_norm, cos_t, sin_t, batch, seq)

    y = merge_branches(o_gdn, o_moba, w_branch_gdn.astype(BF16), w_branch_moba.astype(BF16), proj_m, gg_off, gm_off)
    x = matmul_residual(y, w_out.astype(BF16), x, 1.0, tm=1024, tn=512)

    return _swiglu_half_step(x, ffn2_norm, ffn2_w_gate, ffn2_w_up, ffn2_w_down)


def kernel(x, ffn1_norm, ffn1_w_gate, ffn1_w_up, ffn1_w_down, mix_norm, w_in, gdn_conv_w, gdn_A_log, gdn_dt_bias,
           gdn_o_norm, moba_q_norm, moba_k_norm, w_branch_gdn, w_branch_moba, w_out, ffn2_norm, ffn2_w_gate,
           ffn2_w_up, ffn2_w_down):
    batch, seq, dm = x.shape
    assert ffn1_norm.shape[0] == 1, "single-layer stack"
    out = _layer(x.reshape(batch * seq, dm), ffn1_norm[0], ffn1_w_gate[0], ffn1_w_up[0], ffn1_w_down[0],
                 mix_norm[0], w_in[0], gdn_conv_w[0], gdn_A_log[0], gdn_dt_bias[0], gdn_o_norm[0],
                 moba_q_norm[0], moba_k_norm[0], w_branch_gdn[0], w_branch_moba[0], w_out[0],
                 ffn2_norm[0], ffn2_w_gate[0], ffn2_w_up[0], ffn2_w_down[0], batch, seq)
    return out.reshape(batch, seq, dm)
```

```python
import functools

import jax
import jax.numpy as jnp
from jax import lax
from jax.experimental import pallas as pl
from jax.experimental.pallas import tpu as pltpu

F32 = jnp.float32
BF16 = jnp.bfloat16

NORM_EPS = 1e-6
HEAD_DIM = 128
GDN_QK_HEADS = 16
GDN_V_HEADS = 32
GDN_CONV = 4
GDN_GROUP = 256
GDN_GROUPS_PER_STEP = 4
MOBA_HEADS = 16
MOBA_HEADS_PER_STEP = 2
MOBA_BLOCK = 256
MOBA_TOP_K = 3
MOBA_CHUNK = 4
MOBA_QBLOCKS_PER_STEP = 2
ROPE_THETA = 10000.0
LOG2_E = 1.4426950408889634
CONV_HALO = 8
SUBLANES = 8

VMEM_LIMIT = 56 * 1024 * 1024


def _params(*sem):
    return pltpu.CompilerParams(dimension_semantics=sem, vmem_limit_bytes=VMEM_LIMIT)


def _sigmoid(x):
    return 1.0 / (1.0 + jnp.exp(-x))


def _dot(a, b):
    return jnp.dot(a, b, preferred_element_type=F32)


def _dot_nt(a, b):
    return lax.dot_general(a, b, (((1,), (1,)), ((), ())), preferred_element_type=F32)


def _dot_tn(a, b):
    return lax.dot_general(a, b, (((0,), (0,)), ((), ())), preferred_element_type=F32)


def _rmsnorm_kernel(x_ref, g_ref, o_ref):
    x = x_ref[...]
    ms = jnp.mean(x * x, axis=-1, keepdims=True)
    o_ref[...] = (x * lax.rsqrt(ms + NORM_EPS) * g_ref[...]).astype(o_ref.dtype)


def rmsnorm_bf16(x, gain, tm=256):
    m, d = x.shape
    return pl.pallas_call(
        _rmsnorm_kernel,
        grid=(m // tm,),
        in_specs=[pl.BlockSpec((tm, d), lambda i: (i, 0)),
                  pl.BlockSpec((1, d), lambda i: (0, 0))],
        out_specs=pl.BlockSpec((tm, d), lambda i: (i, 0)),
        out_shape=jax.ShapeDtypeStruct((m, d), BF16),
        compiler_params=_params("parallel"),
        name="rmsnorm",
    )(x, gain.reshape(1, d))


def _ffn_up_kernel(h_ref, wg_ref, wu_ref, o_ref):
    h = h_ref[...]
    g = _dot(h, wg_ref[...].astype(BF16))
    u = _dot(h, wu_ref[...].astype(BF16))
    o_ref[...] = (g * _sigmoid(g) * u).astype(o_ref.dtype)


def ffn_up(h, wg, wu, tm=1024, tn=256):
    m, k = h.shape
    n = wg.shape[1]
    return pl.pallas_call(
        _ffn_up_kernel,
        grid=(m // tm, n // tn),
        in_specs=[pl.BlockSpec((tm, k), lambda i, j: (i, 0)),
                  pl.BlockSpec((k, tn), lambda i, j: (0, j)),
                  pl.BlockSpec((k, tn), lambda i, j: (0, j))],
        out_specs=pl.BlockSpec((tm, tn), lambda i, j: (i, j)),
        out_shape=jax.ShapeDtypeStruct((m, n), BF16),
        compiler_params=_params("parallel", "arbitrary"),
        name="ffn_up",
    )(h, wg, wu)


def _mm_kernel(a_ref, b_ref, o_ref):
    o_ref[...] = _dot(a_ref[...], b_ref[...].astype(BF16)).astype(o_ref.dtype)


def matmul(a, b, n, out_dtype, tm=1024, tn=512):
    m, k = a.shape
    tn = min(tn, n)
    return pl.pallas_call(
        _mm_kernel,
        grid=(m // tm, n // tn),
        in_specs=[pl.BlockSpec((tm, k), lambda i, j: (i, 0)),
                  pl.BlockSpec((k, tn), lambda i, j: (0, j))],
        out_specs=pl.BlockSpec((tm, tn), lambda i, j: (i, j)),
        out_shape=jax.ShapeDtypeStruct((m, n), out_dtype),
        compiler_params=_params("parallel", "arbitrary"),
        name="matmul",
    )(a, b)


def _mm_residual_kernel(a_ref, b_ref, r_ref, o_ref, *, scale):
    o_ref[...] = r_ref[...] + scale * _dot(a_ref[...], b_ref[...])


def matmul_residual(a, b, res, scale, tm, tn):
    m, k = a.shape
    n = b.shape[1]
    return pl.pallas_call(
        functools.partial(_mm_residual_kernel, scale=scale),
        grid=(m // tm, n // tn),
        in_specs=[pl.BlockSpec((tm, k), lambda i, j: (i, 0)),
                  pl.BlockSpec((k, tn), lambda i, j: (0, j)),
                  pl.BlockSpec((tm, tn), lambda i, j: (i, j))],
        out_specs=pl.BlockSpec((tm, tn), lambda i, j: (i, j)),
        out_shape=jax.ShapeDtypeStruct((m, n), F32),
        compiler_params=_params("parallel", "arbitrary"),
        name="matmul_residual",
    )(a, b, res)


def _merge_kernel(og_ref, om_ref, wg_ref, wm_ref, gg_ref, gm_ref, o_ref):
    yg = _dot(og_ref[...], wg_ref[...])
    ym = _dot(om_ref[...], wm_ref[...])
    o_ref[...] = (_sigmoid(gg_ref[...]) * yg + _sigmoid(gm_ref[...]) * ym).astype(o_ref.dtype)


def merge_branches(o_gdn, o_moba, wg, wm, proj, gg_off, gm_off, tm=1024, tn=512):
    m, kg = o_gdn.shape
    km = o_moba.shape[1]
    n = wg.shape[1]
    ggb, gmb = gg_off // tn, gm_off // tn
    return pl.pallas_call(
        _merge_kernel,
        grid=(m // tm, n // tn),
        in_specs=[pl.BlockSpec((tm, kg), lambda i, j: (i, 0)),
                  pl.BlockSpec((tm, km), lambda i, j: (i, 0)),
                  pl.BlockSpec((kg, tn), lambda i, j: (0, j)),
                  pl.BlockSpec((km, tn), lambda i, j: (0, j)),
                  pl.BlockSpec((tm, tn), lambda i, j: (i, ggb + j)),
                  pl.BlockSpec((tm, tn), lambda i, j: (i, gmb + j))],
        out_specs=pl.BlockSpec((tm, tn), lambda i, j: (i, j)),
        out_shape=jax.ShapeDtypeStruct((m, n), BF16),
        compiler_params=_params("parallel", "arbitrary"),
        name="merge_branches",
    )(o_gdn, o_moba, wg, wm, proj, proj)


def _split_bf16(a):
    hi = a.astype(BF16)
    lo = (a - hi.astype(F32)).astype(BF16)
    return hi, lo


def _mm3(a_split, b_split):
    ah, al = a_split
    bh, bl = b_split
    lhs = jnp.concatenate([ah, ah, al], axis=1)
    rhs = jnp.concatenate([bh, bl, bh], axis=0)
    return _dot(lhs, rhs)


def _odd_blocks(x, n):
    return jnp.concatenate([x[(2 * i + 1) * n:(2 * i + 2) * n] for i in range(x.shape[0] // (2 * n))], axis=0)


def _zero_even_blocks(y, n):
    zeros = jnp.zeros((n, y.shape[1]), y.dtype)
    pieces = []
    for i in range(y.shape[0] // n):
        pieces += [zeros, y[i * n:(i + 1) * n]]
    return jnp.concatenate(pieces, axis=0)


def _sub_from_odd_blocks(x, y, n):
    pieces = []
    for i in range(x.shape[0] // (2 * n)):
        pieces += [x[2 * i * n:(2 * i + 1) * n], x[(2 * i + 1) * n:(2 * i + 2) * n] - y[i * n:(i + 1) * n]]
    return jnp.concatenate(pieces, axis=0)


def _unit_lower_inverses(lmats, eye, ij_xor):
    size = eye.shape[0]
    dinvs = [eye - jnp.where(ij_xor == 1, lm, 0.0) for lm in lmats]
    level = 1
    while (2 << level) <= size:
        n = 1 << level
        mask = (ij_xor >> level) == 1
        d16 = [dv.astype(BF16) for dv in dinvs]
        if n < SUBLANES:
            cd = [_dot(jnp.where(mask, lm, 0.0).astype(BF16), dh) for lm, dh in zip(lmats, d16)]
            dinvs = [dv - _dot(dh, m.astype(BF16)) for dv, dh, m in zip(dinvs, d16, cd)]
        else:
            mask_odd = (_odd_blocks(ij_xor, n) >> level) == 1
            cd = [_dot(jnp.where(mask_odd, _odd_blocks(lm, n), 0.0).astype(BF16), dh) for lm, dh in zip(lmats, d16)]
            upd = [_dot(_odd_blocks(dv, n).astype(BF16), _zero_even_blocks(m, n).astype(BF16))
                   for dv, m in zip(dinvs, cd)]
            dinvs = [_sub_from_odd_blocks(dv, u, n) for dv, u in zip(dinvs, upd)]
        level += 1
    resid = [eye - dv - _mm3(_split_bf16(lm), _split_bf16(dv)) for lm, dv in zip(lmats, dinvs)]
    return [dv + _dot(dv.astype(BF16), r.astype(BF16)) for dv, r in zip(dinvs, resid)]


def _conv_silu(x_ref, w_ref, r0, rows):
    x = x_ref[pl.ds(r0, rows), :]
    prev = x_ref[pl.ds(pl.multiple_of(jnp.maximum(r0 - CONV_HALO, 0), CONV_HALO), CONV_HALO), :]
    xx = jnp.concatenate([jnp.where(r0 > 0, prev, 0.0), x], axis=0)
    acc = x * w_ref[GDN_CONV - 1:GDN_CONV, :]
    for j in range(GDN_CONV - 1):
        acc = acc + pltpu.roll(xx, GDN_CONV - 1 - j, axis=0)[CONV_HALO:CONV_HALO + rows] * w_ref[j:j + 1, :]
    return acc * _sigmoid(acc)


def _l2_normalize(x):
    return x * lax.rsqrt(jnp.sum(x * x, axis=-1, keepdims=True) + NORM_EPS)


def _gdn_kernel(q_ref, k_ref, v_ref, z_ref, wq_ref, wk_ref, wv_ref, a_ref, b_ref, alog_ref, dtb_ref, onorm_ref,
                o_ref, *, seq):
    c = GDN_GROUP
    d = HEAD_DIM
    heads = v_ref.shape[1] // d
    ngrp = GDN_GROUPS_PER_STEP
    ii = lax.broadcasted_iota(jnp.int32, (c, c), 0)
    jj = lax.broadcasted_iota(jnp.int32, (c, c), 1)
    lower = ii >= jj
    strict = ii > jj
    eye_b = ii == jj
    eye = eye_b.astype(F32)
    ij_xor = ii ^ jj
    scale = d ** -0.5

    def step(it, states):
        rows, qs, ks, vs, kks, qks = [], [], [], [], [], []
        for gg in range(ngrp):
            r0 = pl.multiple_of((it * ngrp + gg) * c, c)
            q = _l2_normalize(_conv_silu(q_ref, wq_ref, r0, c)) * scale
            k = _l2_normalize(_conv_silu(k_ref, wk_ref, r0, c))
            k16 = k.astype(BF16)
            rows.append(r0)
            qs.append(q)
            ks.append(k)
            vs.append(_conv_silu(v_ref, wv_ref, r0, c))
            kks.append(_dot_nt(k16, k16))
            qks.append(_dot_nt(q.astype(BF16), k16))
        prob = []
        for gg in range(ngrp):
            for h in range(heads):
                a_row = a_ref[h:h + 1, pl.ds(rows[gg], c)]
                b_row = b_ref[h:h + 1, pl.ds(rows[gg], c)]
                x = a_row + dtb_ref[h:h + 1, :]
                softplus = jnp.maximum(x, 0.0) + jnp.log1p(jnp.exp(-jnp.abs(x)))
                g_row = -jnp.exp(alog_ref[h:h + 1, :]) * softplus
                beta_row = _sigmoid(b_row)
                gc_col = jnp.sum(jnp.where(lower, g_row, 0.0), axis=1, keepdims=True)
                gc_row = jnp.sum(jnp.where(eye_b, gc_col, 0.0), axis=0, keepdims=True)
                beta_col = jnp.sum(jnp.where(eye_b, beta_row, 0.0), axis=1, keepdims=True)
                decay = jnp.where(lower, jnp.exp(jnp.where(lower, gc_col - gc_row, 0.0)), 0.0)
                lmat = jnp.where(strict, beta_col * kks[gg] * decay, 0.0)
                prob.append(dict(gg=gg, h=h, gc_col=gc_col, beta_col=beta_col, decay=decay, lmat=lmat))
        tinv = _unit_lower_inverses([p["lmat"] for p in prob], eye, ij_xor)
        for p, t in zip(prob, tinv):
            gg, h, gc_col = p["gg"], p["h"], p["gc_col"]
            v = vs[gg][:, h * d:(h + 1) * d]
            rhs = p["beta_col"] * jnp.concatenate([v, ks[gg] * jnp.exp(gc_col)], axis=1)
            uw = _dot(t.astype(BF16), rhs.astype(BF16))
            p["u"] = uw[:, :d]
            p["w16"] = uw[:, d:].astype(BF16)
            p["attn16"] = jnp.where(lower, qks[gg] * p["decay"], 0.0).astype(BF16)
            p["qg16"] = (qs[gg] * jnp.exp(gc_col)).astype(BF16)
            g_last = gc_col[c - 1:c, :]
            p["kdec16"] = (ks[gg] * jnp.exp(g_last - gc_col)).astype(BF16)
            p["carry"] = jnp.exp(g_last)
        states = list(states)
        for p in prob:
            gg, h = p["gg"], p["h"]
            s = states[h]
            s16 = s.astype(BF16)
            v_new = p["u"] - _dot(p["w16"], s16)
            vn16 = v_new.astype(BF16)
            o = _dot(p["qg16"], s16) + _dot(p["attn16"], vn16)
            states[h] = s * p["carry"] + _dot_tn(p["kdec16"], vn16)
            ms = jnp.mean(o * o, axis=-1, keepdims=True)
            z = z_ref[pl.ds(rows[gg], c), h * d:(h + 1) * d]
            y = o * lax.rsqrt(ms + NORM_EPS) * onorm_ref[...] * (z * _sigmoid(z))
            o_ref[pl.ds(rows[gg], c), h * d:(h + 1) * d] = y.astype(o_ref.dtype)
        return tuple(states)

    init = tuple(jnp.zeros((d, d), F32) for _ in range(heads))
    lax.fori_loop(0, seq // (c * ngrp), step, init)


def gdn_core(proj, conv_w, a_t, b_t, alog, dtb, o_norm, batch, seq):
    d = HEAD_DIM
    rep = GDN_V_HEADS // GDN_QK_HEADS
    m = proj.shape[0]
    kb = GDN_QK_HEADS
    vb = 2 * GDN_QK_HEADS // rep
    zb = vb + GDN_V_HEADS // rep
    assert seq % (GDN_GROUP * GDN_GROUPS_PER_STEP) == 0
    return pl.pallas_call(
        functools.partial(_gdn_kernel, seq=seq),
        grid=(batch, GDN_QK_HEADS),
        in_specs=[pl.BlockSpec((seq, d), lambda b, j: (b, j)),
                  pl.BlockSpec((seq, d), lambda b, j: (b, kb + j)),
                  pl.BlockSpec((seq, rep * d), lambda b, j: (b, vb + j)),
                  pl.BlockSpec((seq, rep * d), lambda b, j: (b, zb + j)),
                  pl.BlockSpec((GDN_CONV, d), lambda b, j: (0, j)),
                  pl.BlockSpec((GDN_CONV, d), lambda b, j: (0, kb + j)),
                  pl.BlockSpec((GDN_CONV, rep * d), lambda b, j: (0, vb + j)),
                  pl.BlockSpec((None, None, rep, seq), lambda b, j: (b, j, 0, 0)),
                  pl.BlockSpec((None, None, rep, seq), lambda b, j: (b, j, 0, 0)),
                  pl.BlockSpec((None, rep, GDN_GROUP), lambda b, j: (j, 0, 0)),
                  pl.BlockSpec((None, rep, GDN_GROUP), lambda b, j: (j, 0, 0)),
                  pl.BlockSpec((1, d), lambda b, j: (0, 0))],
        out_specs=pl.BlockSpec((seq, rep * d), lambda b, j: (b, j)),
        out_shape=jax.ShapeDtypeStruct((m, GDN_V_HEADS * d), BF16),
        compiler_params=_params("parallel", "parallel"),
        name="gdn_core",
    )(proj, proj, proj, proj, conv_w, conv_w, conv_w, a_t, b_t, alog, dtb, o_norm.reshape(1, d))


def _norm_rope(x, gain, cos, sin):
    ms = jnp.mean(x * x, axis=-1, keepdims=True)
    y = x * lax.rsqrt(ms + NORM_EPS) * gain
    return y * cos + pltpu.roll(y, HEAD_DIM // 2, axis=1) * sin


def _moba_kernel(q_ref, k_ref, v_ref, qgain_ref, kgain_ref, cos_ref, sin_ref, o_ref,
                 kpad_ref, vt_ref, kmean_ref, sel_ref, *, nblk):
    blk = MOBA_BLOCK
    d = HEAD_DIM
    cb = MOBA_CHUNK
    pad = cb - 1
    heads = q_ref.shape[1] // d
    scale = d ** -0.5 * LOG2_E
    sel0 = SUBLANES

    for h in range(heads):
        kpad_ref[h, 0:pad * blk, :] = jnp.zeros((pad * blk, d), BF16)
        vt_ref[h, 0:pad] = jnp.zeros((pad, d, blk), BF16)
        for e in range(MOBA_QBLOCKS_PER_STEP):
            sel_ref[e * heads + h, 0:sel0, :] = jnp.zeros((sel0, blk), F32)
        for n in range(nblk):
            kb = _norm_rope(k_ref[n * blk:(n + 1) * blk, h * d:(h + 1) * d], kgain_ref[...],
                            cos_ref[n * blk:(n + 1) * blk, :], sin_ref[n * blk:(n + 1) * blk, :])
            kmean_ref[h, n:n + 1, :] = jnp.mean(kb, axis=0, keepdims=True)
            kpad_ref[h, (pad + n) * blk:(pad + n + 1) * blk, :] = kb.astype(BF16)
            vt_ref[h, pad + n] = v_ref[n * blk:(n + 1) * blk, h * d:(h + 1) * d].T.astype(BF16)

    nidx = lax.broadcasted_iota(jnp.int32, (nblk, blk), 0)
    kpos = lax.broadcasted_iota(jnp.int32, (blk, blk), 0)
    qpos = lax.broadcasted_iota(jnp.int32, (blk, blk), 1)
    causal = kpos <= qpos

    def chunk_scores(h, ln, q16, first_blk, own):
        k0 = pl.multiple_of((first_blk + pad) * blk, blk)
        st = _dot_nt(kpad_ref[h, pl.ds(k0, cb * blk), :], q16)
        parts = []
        for u in range(cb):
            su = st[u * blk:(u + 1) * blk, :]
            if own and u == cb - 1:
                keep = causal
            else:
                keep = sel_ref[ln, pl.ds(sel0 + first_blk + u, 1), :] > 0.0
            parts.append(jnp.where(keep, su, -jnp.inf))
        return parts

    def chunk_pv(h, first_blk, p_parts):
        acc = None
        for u in range(cb):
            term = _dot(vt_ref[h, first_blk + pad + u], p_parts[u].astype(BF16))
            acc = term if acc is None else acc + term
        return acc

    qpp = MOBA_QBLOCKS_PER_STEP
    lanes = [(h, e) for e in range(qpp) for h in range(heads)]
    ls = range(len(lanes))

    def qblocks(qp, carry):
        qbs = [qp * qpp + e for _, e in lanes]
        rows = [pl.multiple_of(qb * blk, blk) for qb in qbs]
        qs = [_norm_rope(q_ref[pl.ds(rows[i], blk), h * d:(h + 1) * d], qgain_ref[...],
                         cos_ref[pl.ds(rows[i], blk), :], sin_ref[pl.ds(rows[i], blk), :])
              for i, (h, _) in enumerate(lanes)]
        q16s = [(q * scale).astype(BF16) for q in qs]
        gates = [lax.dot_general(kmean_ref[h], qs[i], (((1,), (1,)), ((), ())),
                                 precision=lax.Precision.HIGHEST, preferred_element_type=F32)
                 for i, (h, _) in enumerate(lanes)]
        for i in ls:
            rank = jnp.zeros((nblk, blk), jnp.int32)
            for m in range(nblk):
                gm = gates[i][m:m + 1, :]
                beats = (gm > gates[i]) | ((gm == gates[i]) & (m < nidx))
                rank = rank + jnp.where(beats & (m < qbs[i]), 1, 0)
            sel_ref[i, sel0:sel0 + nblk, :] = jnp.where((nidx < qbs[i]) & (rank < MOBA_TOP_K), 1.0, 0.0)
        parts0 = [chunk_scores(h, i, q16s[i], qbs[i] - pad, own=True) for i, (h, _) in enumerate(lanes)]
        m0 = [functools.reduce(jnp.maximum, [jnp.max(s, axis=0, keepdims=True) for s in parts0[i]]) for i in ls]
        p0 = [[jnp.exp2(s - m0[i]) for s in parts0[i]] for i in ls]
        l0 = [functools.reduce(jnp.add, [jnp.sum(p, axis=0, keepdims=True) for p in p0[i]]) for i in ls]
        stats = [(m0[i], l0[i], chunk_pv(h, qbs[i] - pad, p0[i])) for i, (h, _) in enumerate(lanes)]

        def older(t, st):
            firsts = [qbs[i] - pad - cb * t for i in ls]
            parts = [chunk_scores(h, i, q16s[i], firsts[i], own=False) for i, (h, _) in enumerate(lanes)]
            m_new = [functools.reduce(jnp.maximum, [st[i][0]] + [jnp.max(s, axis=0, keepdims=True) for s in parts[i]])
                     for i in ls]
            alpha = [jnp.exp2(st[i][0] - m_new[i]) for i in ls]
            p_parts = [[jnp.exp2(s - m_new[i]) for s in parts[i]] for i in ls]
            l_new = [alpha[i] * st[i][1] + functools.reduce(jnp.add, [jnp.sum(p, axis=0, keepdims=True) for p in p_parts[i]])
                     for i in ls]
            acc = [alpha[i] * st[i][2] + chunk_pv(h, firsts[i], p_parts[i]) for i, (h, _) in enumerate(lanes)]
            return tuple((m_new[i], l_new[i], acc[i]) for i in ls)

        final = lax.fori_loop(1, (qp * qpp) // cb + 1, older, tuple(stats))
        for i, (h, _) in enumerate(lanes):
            _, l_f, acc_f = final[i]
            o_ref[pl.ds(rows[i], blk), h * d:(h + 1) * d] = (acc_f / l_f).T.astype(o_ref.dtype)
        return carry

    lax.fori_loop(0, nblk // qpp, qblocks, 0)


def moba_core(proj, q_off, k_off, v_off, q_gain, k_gain, cos_t, sin_t, batch, seq):
    d = HEAD_DIM
    hp = MOBA_HEADS_PER_STEP
    m = proj.shape[0]
    nblk = seq // MOBA_BLOCK
    pad = MOBA_CHUNK - 1
    assert MOBA_CHUNK % MOBA_QBLOCKS_PER_STEP == 0 and nblk % MOBA_QBLOCKS_PER_STEP == 0
    qb, kb, vb = q_off // (hp * d), k_off // (hp * d), v_off // (hp * d)
    return pl.pallas_call(
        functools.partial(_moba_kernel, nblk=nblk),
        grid=(batch, MOBA_HEADS // hp),
        in_specs=[pl.BlockSpec((seq, hp * d), lambda b, h: (b, qb + h)),
                  pl.BlockSpec((seq, hp * d), lambda b, h: (b, kb + h)),
                  pl.BlockSpec((seq, hp * d), lambda b, h: (b, vb + h)),
                  pl.BlockSpec((1, d), lambda b, h: (0, 0)),
                  pl.BlockSpec((1, d), lambda b, h: (0, 0)),
                  pl.BlockSpec((seq, d), lambda b, h: (0, 0)),
                  pl.BlockSpec((seq, d), lambda b, h: (0, 0))],
        out_specs=pl.BlockSpec((seq, hp * d), lambda b, h: (b, h)),
        out_shape=jax.ShapeDtypeStruct((m, MOBA_HEADS * d), BF16),
        scratch_shapes=[pltpu.VMEM((hp, (nblk + pad) * MOBA_BLOCK, d), BF16),
                        pltpu.VMEM((hp, nblk + pad, d, MOBA_BLOCK), BF16),
                        pltpu.VMEM((hp, nblk, d), F32),
                        pltpu.VMEM((hp * MOBA_QBLOCKS_PER_STEP, SUBLANES + nblk, MOBA_BLOCK), F32)],
        compiler_params=_params("parallel", "parallel"),
        name="moba_core",
    )(proj, proj, proj, q_gain.reshape(1, d), k_gain.reshape(1, d), cos_t, sin_t)


def _swiglu_half_step(x, norm, w_gate, w_up, w_down):
    h = rmsnorm_bf16(x, norm)
    act = ffn_up(h, w_gate, w_up)
    return matmul_residual(act, w_down.astype(BF16), x, 0.5, tm=512, tn=512)


def _rope_tables(seq):
    half = HEAD_DIM // 2
    inv_freq = ROPE_THETA ** (-jnp.arange(half, dtype=F32) / half)
    ang = jnp.arange(seq).astype(F32)[:, None] * inv_freq[None, :]
    cos, sin = jnp.cos(ang), jnp.sin(ang)
    return jnp.concatenate([cos, cos], axis=1), jnp.concatenate([-sin, sin], axis=1)


def _layer(x, ffn1_norm, ffn1_w_gate, ffn1_w_up, ffn1_w_down, mix_norm, w_in, gdn_conv_w, gdn_A_log,
           gdn_dt_bias, gdn_o_norm, moba_q_norm, moba_k_norm, w_branch_gdn, w_branch_moba, w_out,
           ffn2_norm, ffn2_w_gate, ffn2_w_up, ffn2_w_down, batch, seq):
    d = HEAD_DIM
    qkw = GDN_QK_HEADS * d
    vw = GDN_V_HEADS * d
    mw = MOBA_HEADS * d
    dm = x.shape[1]

    x = _swiglu_half_step(x, ffn1_norm, ffn1_w_gate, ffn1_w_up, ffn1_w_down)

    ab_off = 2 * qkw + 2 * vw
    w_ab = jnp.pad(w_in[:, ab_off:ab_off + 2 * GDN_V_HEADS], ((0, 0), (0, d - 2 * GDN_V_HEADS))).astype(BF16)
    w_in16 = w_in.astype(BF16)
    w_rest = w_in16[:, ab_off + 2 * GDN_V_HEADS:]
    mq_off, mk_off, mv_off, gg_off, gm_off = 0, mw, 2 * mw, 3 * mw, 3 * mw + dm

    h = rmsnorm_bf16(x, mix_norm)
    proj_g = matmul(h, w_in16, ab_off, F32)
    ab = matmul(h, w_ab, d, F32)
    proj_m = matmul(h, w_rest, w_rest.shape[1], F32)

    rep = GDN_V_HEADS // GDN_QK_HEADS
    to_heads = lambda t: t.reshape(batch, seq, GDN_QK_HEADS, rep).transpose(0, 2, 3, 1)
    a_t = to_heads(ab[:, :GDN_V_HEADS])
    b_t = to_heads(ab[:, GDN_V_HEADS:2 * GDN_V_HEADS])
    per_head = lambda t: jnp.broadcast_to(t.astype(F32).reshape(GDN_QK_HEADS, rep, 1), (GDN_QK_HEADS, rep, GDN_GROUP))
    o_gdn = gdn_core(proj_g, gdn_conv_w, a_t, b_t, per_head(gdn_A_log), per_head(gdn_dt_bias), gdn_o_norm, batch, seq)

    cos_t, sin_t = _rope_tables(seq)
    o_moba = moba_core(proj_m, mq_off, mk_off, mv_off, moba_q_norm, moba_k_norm, cos_t, sin_t, batch, seq)

    y = merge_branches(o_gdn, o_moba, w_branch_gdn.astype(BF16), w_branch_moba.astype(BF16), proj_m, gg_off, gm_off)
    x = matmul_residual(y, w_out.astype(BF16), x, 1.0, tm=1024, tn=512)

    return _swiglu_half_step(x, ffn2_norm, ffn2_w_gate, ffn2_w_up, ffn2_w_down)


def kernel(x, ffn1_norm, ffn1_w_gate, ffn1_w_up, ffn1_w_down, mix_norm, w_in, gdn_conv_w, gdn_A_log, gdn_dt_bias,
           gdn_o_norm, moba_q_norm, moba_k_norm, w_branch_gdn, w_branch_moba, w_out, ffn2_norm, ffn2_w_gate,
           ffn2_w_up, ffn2_w_down):
    batch, seq, dm = x.shape
    assert ffn1_norm.shape[0] == 1, "single-layer stack"
    out = _layer(x.reshape(batch * seq, dm), ffn1_norm[0], ffn1_w_gate[0], ffn1_w_up[0], ffn1_w_down[0],
                 mix_norm[0], w_in[0], gdn_conv_w[0], gdn_A_log[0], gdn_dt_bias[0], gdn_o_norm[0],
                 moba_q_norm[0], moba_k_norm[0], w_branch_gdn[0], w_branch_moba[0], w_out[0],
                 ffn2_norm[0], ffn2_w_gate[0], ffn2_w_up[0], ffn2_w_down[0], batch, seq)
    return out.reshape(batch, seq, dm)
```

```python
import functools

import jax
import jax.numpy as jnp
from jax import lax
from jax.experimental import pallas as pl
from jax.experimental.pallas import tpu as pltpu

F32 = jnp.float32
BF16 = jnp.bfloat16

NORM_EPS = 1e-6
HEAD_DIM = 128
GDN_QK_HEADS = 16
GDN_V_HEADS = 32
GDN_CONV = 4
GDN_GROUP = 256
GDN_GROUPS_PER_STEP = 4
MOBA_HEADS = 16
MOBA_HEADS_PER_STEP = 2
MOBA_BLOCK = 256
MOBA_TOP_K = 3
MOBA_CHUNK = 4
MOBA_QBLOCKS_PER_STEP = 2
ROPE_THETA = 10000.0
LOG2_E = 1.4426950408889634
CONV_HALO = 8
SUBLANES = 8

VMEM_LIMIT = 56 * 1024 * 1024


def _params(*sem):
    return pltpu.CompilerParams(dimension_semantics=sem, vmem_limit_bytes=VMEM_LIMIT)


def _sigmoid(x):
    return 1.0 / (1.0 + jnp.exp(-x))


def _dot(a, b):
    return jnp.dot(a, b, preferred_element_type=F32)


def _dot_nt(a, b):
    return lax.dot_general(a, b, (((1,), (1,)), ((), ())), preferred_element_type=F32)


def _dot_tn(a, b):
    return lax.dot_general(a, b, (((0,), (0,)), ((), ())), preferred_element_type=F32)


def _rmsnorm_kernel(x_ref, g_ref, o_ref):
    x = x_ref[...]
    ms = jnp.mean(x * x, axis=-1, keepdims=True)
    o_ref[...] = (x * lax.rsqrt(ms + NORM_EPS) * g_ref[...]).astype(o_ref.dtype)


def rmsnorm_bf16(x, gain, tm=512):
    m, d = x.shape
    return pl.pallas_call(
        _rmsnorm_kernel,
        grid=(m // tm,),
        in_specs=[pl.BlockSpec((tm, d), lambda i: (i, 0)),
                  pl.BlockSpec((1, d), lambda i: (0, 0))],
        out_specs=pl.BlockSpec((tm, d), lambda i: (i, 0)),
        out_shape=jax.ShapeDtypeStruct((m, d), BF16),
        compiler_params=_params("parallel"),
        name="rmsnorm",
    )(x, gain.reshape(1, d))


def _ffn_up_kernel(h_ref, wg_ref, wu_ref, o_ref):
    h = h_ref[...]
    g = _dot(h, wg_ref[...].astype(BF16))
    u = _dot(h, wu_ref[...].astype(BF16))
    o_ref[...] = (g * _sigmoid(g) * u).astype(o_ref.dtype)


def ffn_up(h, wg, wu, tm=1024, tn=256):
    m, k = h.shape
    n = wg.shape[1]
    return pl.pallas_call(
        _ffn_up_kernel,
        grid=(m // tm, n // tn),
        in_specs=[pl.BlockSpec((tm, k), lambda i, j: (i, 0)),
                  pl.BlockSpec((k, tn), lambda i, j: (0, j)),
                  pl.BlockSpec((k, tn), lambda i, j: (0, j))],
        out_specs=pl.BlockSpec((tm, tn), lambda i, j: (i, j)),
        out_shape=jax.ShapeDtypeStruct((m, n), BF16),
        compiler_params=_params("parallel", "arbitrary"),
        name="ffn_up",
    )(h, wg, wu)


def _mm_kernel(a_ref, b_ref, o_ref):
    o_ref[...] = _dot(a_ref[...], b_ref[...].astype(BF16)).astype(o_ref.dtype)


def matmul(a, b, n, out_dtype, tm=1024, tn=512):
    m, k = a.shape
    tn = min(tn, n)
    return pl.pallas_call(
        _mm_kernel,
        grid=(m // tm, n // tn),
        in_specs=[pl.BlockSpec((tm, k), lambda i, j: (i, 0)),
                  pl.BlockSpec((k, tn), lambda i, j: (0, j))],
        out_specs=pl.BlockSpec((tm, tn), lambda i, j: (i, j)),
        out_shape=jax.ShapeDtypeStruct((m, n), out_dtype),
        compiler_params=_params("parallel", "arbitrary"),
        name="matmul",
    )(a, b)


def _mm_nt_kernel(a_ref, bt_ref, o_ref):
    o_ref[...] = _dot_nt(a_ref[...], bt_ref[...].astype(BF16)).astype(o_ref.dtype)


def matmul_nt(a, bt, row0, n, out_dtype, tm=1024, tn=512):
    m, k = a.shape
    assert row0 % SUBLANES == 0 and n % tn == 0
    return pl.pallas_call(
        _mm_nt_kernel,
        grid=(m // tm, n // tn),
        in_specs=[pl.BlockSpec((tm, k), lambda i, j: (i, 0)),
                  pl.BlockSpec((pl.Element(tn), pl.Element(k)),
                               lambda i, j: (pl.multiple_of(row0 + j * tn, SUBLANES), 0))],
        out_specs=pl.BlockSpec((tm, tn), lambda i, j: (i, j)),
        out_shape=jax.ShapeDtypeStruct((m, n), out_dtype),
        compiler_params=_params("parallel", "arbitrary"),
        name="matmul_nt",
    )(a, bt)


def _mm_residual_kernel(a_ref, b_ref, r_ref, o_ref, *, scale):
    o_ref[...] = r_ref[...] + scale * _dot(a_ref[...], b_ref[...].astype(BF16))


def matmul_residual(a, b, res, scale, tm, tn):
    m, k = a.shape
    n = b.shape[1]
    return pl.pallas_call(
        functools.partial(_mm_residual_kernel, scale=scale),
        grid=(m // tm, n // tn),
        in_specs=[pl.BlockSpec((tm, k), lambda i, j: (i, 0)),
                  pl.BlockSpec((k, tn), lambda i, j: (0, j)),
                  pl.BlockSpec((tm, tn), lambda i, j: (i, j))],
        out_specs=pl.BlockSpec((tm, tn), lambda i, j: (i, j)),
        out_shape=jax.ShapeDtypeStruct((m, n), F32),
        compiler_params=_params("parallel", "arbitrary"),
        name="matmul_residual",
    )(a, b, res)


def _merge_kernel(og_ref, om_ref, wg_ref, wm_ref, gg_ref, gm_ref, o_ref):
    yg = _dot(og_ref[...], wg_ref[...])
    ym = _dot(om_ref[...], wm_ref[...])
    o_ref[...] = (_sigmoid(gg_ref[...]) * yg + _sigmoid(gm_ref[...]) * ym).astype(o_ref.dtype)


def merge_branches(o_gdn, o_moba, wg, wm, proj, gg_off, gm_off, tm=1024, tn=512):
    m, kg = o_gdn.shape
    km = o_moba.shape[1]
    n = wg.shape[1]
    ggb, gmb = gg_off // tn, gm_off // tn
    return pl.pallas_call(
        _merge_kernel,
        grid=(m // tm, n // tn),
        in_specs=[pl.BlockSpec((tm, kg), lambda i, j: (i, 0)),
                  pl.BlockSpec((tm, km), lambda i, j: (i, 0)),
                  pl.BlockSpec((kg, tn), lambda i, j: (0, j)),
                  pl.BlockSpec((km, tn), lambda i, j: (0, j)),
                  pl.BlockSpec((tm, tn), lambda i, j: (i, ggb + j)),
                  pl.BlockSpec((tm, tn), lambda i, j: (i, gmb + j))],
        out_specs=pl.BlockSpec((tm, tn), lambda i, j: (i, j)),
        out_shape=jax.ShapeDtypeStruct((m, n), BF16),
        compiler_params=_params("parallel", "arbitrary"),
        name="merge_branches",
    )(o_gdn, o_moba, wg, wm, proj, proj)


def _split_bf16(a):
    hi = a.astype(BF16)
    lo = (a - hi.astype(F32)).astype(BF16)
    return hi, lo


def _mm3(a_split, b_split):
    ah, al = a_split
    bh, bl = b_split
    lhs = jnp.concatenate([ah, ah, al], axis=1)
    rhs = jnp.concatenate([bh, bl, bh], axis=0)
    return _dot(lhs, rhs)


def _odd_blocks(x, n):
    return jnp.concatenate([x[(2 * i + 1) * n:(2 * i + 2) * n] for i in range(x.shape[0] // (2 * n))], axis=0)


def _zero_even_blocks(y, n):
    zeros = jnp.zeros((n, y.shape[1]), y.dtype)
    pieces = []
    for i in range(y.shape[0] // n):
        pieces += [zeros, y[i * n:(i + 1) * n]]
    return jnp.concatenate(pieces, axis=0)


def _sub_from_odd_blocks(x, y, n):
    pieces = []
    for i in range(x.shape[0] // (2 * n)):
        pieces += [x[2 * i * n:(2 * i + 1) * n], x[(2 * i + 1) * n:(2 * i + 2) * n] - y[i * n:(i + 1) * n]]
    return jnp.concatenate(pieces, axis=0)


def _unit_lower_inverses(lmats, eye, ij_xor):
    size = eye.shape[0]
    dinvs = [eye - jnp.where(ij_xor == 1, lm, 0.0) for lm in lmats]
    level = 1
    while (2 << level) <= size:
        n = 1 << level
        mask = (ij_xor >> level) == 1
        d16 = [dv.astype(BF16) for dv in dinvs]
        if n < SUBLANES:
            cd = [_dot(jnp.where(mask, lm, 0.0).astype(BF16), dh) for lm, dh in zip(lmats, d16)]
            dinvs = [dv - _dot(dh, m.astype(BF16)) for dv, dh, m in zip(dinvs, d16, cd)]
        else:
            mask_odd = (_odd_blocks(ij_xor, n) >> level) == 1
            cd = [_dot(jnp.where(mask_odd, _odd_blocks(lm, n), 0.0).astype(BF16), dh) for lm, dh in zip(lmats, d16)]
            upd = [_dot(_odd_blocks(dv, n).astype(BF16), _zero_even_blocks(m, n).astype(BF16))
                   for dv, m in zip(dinvs, cd)]
            dinvs = [_sub_from_odd_blocks(dv, u, n) for dv, u in zip(dinvs, upd)]
        level += 1
    resid = [eye - dv - _mm3(_split_bf16(lm), _split_bf16(dv)) for lm, dv in zip(lmats, dinvs)]
    return [dv + _dot(dv.astype(BF16), r.astype(BF16)) for dv, r in zip(dinvs, resid)]


def _conv_silu(x_ref, w_ref, r0, rows):
    x = x_ref[pl.ds(r0, rows), :]
    prev = x_ref[pl.ds(pl.multiple_of(jnp.maximum(r0 - CONV_HALO, 0), CONV_HALO), CONV_HALO), :]
    xx = jnp.concatenate([jnp.where(r0 > 0, prev, 0.0), x], axis=0)
    acc = x * w_ref[GDN_CONV - 1:GDN_CONV, :]
    for j in range(GDN_CONV - 1):
        acc = acc + pltpu.roll(xx, GDN_CONV - 1 - j, axis=0)[CONV_HALO:CONV_HALO + rows] * w_ref[j:j + 1, :]
    return acc * _sigmoid(acc)


def _l2_normalize(x):
    return x * lax.rsqrt(jnp.sum(x * x, axis=-1, keepdims=True) + NORM_EPS)


def _gdn_kernel(q_ref, k_ref, v_ref, z_ref, wq_ref, wk_ref, wv_ref, a_ref, b_ref, alog_ref, dtb_ref, onorm_ref,
                o_ref, *, seq):
    c = GDN_GROUP
    d = HEAD_DIM
    heads = v_ref.shape[1] // d
    ngrp = GDN_GROUPS_PER_STEP
    ii = lax.broadcasted_iota(jnp.int32, (c, c), 0)
    jj = lax.broadcasted_iota(jnp.int32, (c, c), 1)
    lower = ii >= jj
    strict = ii > jj
    eye_b = ii == jj
    eye = eye_b.astype(F32)
    ij_xor = ii ^ jj
    scale = d ** -0.5

    def step(it, states):
        rows, qs, ks, vs, kks, qks = [], [], [], [], [], []
        for gg in range(ngrp):
            r0 = pl.multiple_of((it * ngrp + gg) * c, c)
            q = _l2_normalize(_conv_silu(q_ref, wq_ref, r0, c)) * scale
            k = _l2_normalize(_conv_silu(k_ref, wk_ref, r0, c))
            k16 = k.astype(BF16)
            rows.append(r0)
            qs.append(q)
            ks.append(k)
            vs.append(_conv_silu(v_ref, wv_ref, r0, c))
            kks.append(_dot_nt(k16, k16))
            qks.append(_dot_nt(q.astype(BF16), k16))
        prob = []
        for gg in range(ngrp):
            for h in range(heads):
                a_row = a_ref[h:h + 1, pl.ds(rows[gg], c)]
                b_row = b_ref[h:h + 1, pl.ds(rows[gg], c)]
                x = a_row + dtb_ref[h:h + 1, :]
                softplus = jnp.maximum(x, 0.0) + jnp.log1p(jnp.exp(-jnp.abs(x)))
                g_row = -jnp.exp(alog_ref[h:h + 1, :]) * softplus
                beta_row = _sigmoid(b_row)
                gc_col = jnp.sum(jnp.where(lower, g_row, 0.0), axis=1, keepdims=True)
                gc_row = jnp.sum(jnp.where(eye_b, gc_col, 0.0), axis=0, keepdims=True)
                beta_col = jnp.sum(jnp.where(eye_b, beta_row, 0.0), axis=1, keepdims=True)
                decay = jnp.where(lower, jnp.exp(jnp.where(lower, gc_col - gc_row, 0.0)), 0.0)
                lmat = jnp.where(strict, beta_col * kks[gg] * decay, 0.0)
                prob.append(dict(gg=gg, h=h, gc_col=gc_col, beta_col=beta_col, decay=decay, lmat=lmat))
        tinv = _unit_lower_inverses([p["lmat"] for p in prob], eye, ij_xor)
        for p, t in zip(prob, tinv):
            gg, h, gc_col = p["gg"], p["h"], p["gc_col"]
            v = vs[gg][:, h * d:(h + 1) * d]
            rhs = p["beta_col"] * jnp.concatenate([v, ks[gg] * jnp.exp(gc_col)], axis=1)
            uw = _dot(t.astype(BF16), rhs.astype(BF16))
            p["u"] = uw[:, :d]
            p["w16"] = uw[:, d:].astype(BF16)
            p["attn16"] = jnp.where(lower, qks[gg] * p["decay"], 0.0).astype(BF16)
            p["qg16"] = (qs[gg] * jnp.exp(gc_col)).astype(BF16)
            g_last = gc_col[c - 1:c, :]
            p["kdec16"] = (ks[gg] * jnp.exp(g_last - gc_col)).astype(BF16)
            p["carry"] = jnp.exp(g_last)
        states = list(states)
        for p in prob:
            gg, h = p["gg"], p["h"]
            s = states[h]
            s16 = s.astype(BF16)
            v_new = p["u"] - _dot(p["w16"], s16)
            vn16 = v_new.astype(BF16)
            o = _dot(p["qg16"], s16) + _dot(p["attn16"], vn16)
            states[h] = s * p["carry"] + _dot_tn(p["kdec16"], vn16)
            ms = jnp.mean(o * o, axis=-1, keepdims=True)
            z = z_ref[pl.ds(rows[gg], c), h * d:(h + 1) * d]
            y = o * lax.rsqrt(ms + NORM_EPS) * onorm_ref[...] * (z * _sigmoid(z))
            o_ref[pl.ds(rows[gg], c), h * d:(h + 1) * d] = y.astype(o_ref.dtype)
        return tuple(states)

    init = tuple(jnp.zeros((d, d), F32) for _ in range(heads))
    lax.fori_loop(0, seq // (c * ngrp), step, init)


def gdn_core(proj, conv_w, a_t, b_t, alog, dtb, o_norm, batch, seq):
    d = HEAD_DIM
    rep = GDN_V_HEADS // GDN_QK_HEADS
    m = proj.shape[0]
    kb = GDN_QK_HEADS
    vb = 2 * GDN_QK_HEADS // rep
    zb = vb + GDN_V_HEADS // rep
    assert seq % (GDN_GROUP * GDN_GROUPS_PER_STEP) == 0
    return pl.pallas_call(
        functools.partial(_gdn_kernel, seq=seq),
        grid=(batch, GDN_QK_HEADS),
        in_specs=[pl.BlockSpec((seq, d), lambda b, j: (b, j)),
                  pl.BlockSpec((seq, d), lambda b, j: (b, kb + j)),
                  pl.BlockSpec((seq, rep * d), lambda b, j: (b, vb + j)),
                  pl.BlockSpec((seq, rep * d), lambda b, j: (b, zb + j)),
                  pl.BlockSpec((GDN_CONV, d), lambda b, j: (0, j)),
                  pl.BlockSpec((GDN_CONV, d), lambda b, j: (0, kb + j)),
                  pl.BlockSpec((GDN_CONV, rep * d), lambda b, j: (0, vb + j)),
                  pl.BlockSpec((None, None, rep, seq), lambda b, j: (b, j, 0, 0)),
                  pl.BlockSpec((None, None, rep, seq), lambda b, j: (b, j, 0, 0)),
                  pl.BlockSpec((None, rep, GDN_GROUP), lambda b, j: (j, 0, 0)),
                  pl.BlockSpec((None, rep, GDN_GROUP), lambda b, j: (j, 0, 0)),
                  pl.BlockSpec((1, d), lambda b, j: (0, 0))],
        out_specs=pl.BlockSpec((seq, rep * d), lambda b, j: (b, j)),
        out_shape=jax.ShapeDtypeStruct((m, GDN_V_HEADS * d), BF16),
        compiler_params=_params("parallel", "parallel"),
        name="gdn_core",
    )(proj, proj, proj, proj, conv_w, conv_w, conv_w, a_t, b_t, alog, dtb, o_norm.reshape(1, d))


def _norm_rope(x, gain, cos, sin):
    ms = jnp.mean(x * x, axis=-1, keepdims=True)
    y = x * lax.rsqrt(ms + NORM_EPS) * gain
    return y * cos + pltpu.roll(y, HEAD_DIM // 2, axis=1) * sin


def _moba_kernel(q_ref, k_ref, v_ref, qgain_ref, kgain_ref, cos_ref, sin_ref, o_ref,
                 kpad_ref, vt_ref, kmean_ref, sel_ref, *, nblk):
    blk = MOBA_BLOCK
    d = HEAD_DIM
    cb = MOBA_CHUNK
    pad = cb - 1
    heads = q_ref.shape[1] // d
    scale = d ** -0.5 * LOG2_E
    sel0 = SUBLANES

    for h in range(heads):
        kpad_ref[h, 0:pad * blk, :] = jnp.zeros((pad * blk, d), BF16)
        vt_ref[h, 0:pad] = jnp.zeros((pad, d, blk), BF16)
        for e in range(MOBA_QBLOCKS_PER_STEP):
            sel_ref[e * heads + h, 0:sel0, :] = jnp.zeros((sel0, blk), F32)
        for n in range(nblk):
            kb = _norm_rope(k_ref[n * blk:(n + 1) * blk, h * d:(h + 1) * d], kgain_ref[...],
                            cos_ref[n * blk:(n + 1) * blk, :], sin_ref[n * blk:(n + 1) * blk, :])
            kmean_ref[h, n:n + 1, :] = jnp.mean(kb, axis=0, keepdims=True)
            kpad_ref[h, (pad + n) * blk:(pad + n + 1) * blk, :] = kb.astype(BF16)
            vt_ref[h, pad + n] = v_ref[n * blk:(n + 1) * blk, h * d:(h + 1) * d].T.astype(BF16)

    nidx = lax.broadcasted_iota(jnp.int32, (nblk, blk), 0)
    kpos = lax.broadcasted_iota(jnp.int32, (blk, blk), 0)
    qpos = lax.broadcasted_iota(jnp.int32, (blk, blk), 1)
    causal = kpos <= qpos

    def chunk_scores(h, ln, q16, first_blk, own):
        k0 = pl.multiple_of((first_blk + pad) * blk, blk)
        st = _dot_nt(kpad_ref[h, pl.ds(k0, cb * blk), :], q16)
        parts = []
        for u in range(cb):
            su = st[u * blk:(u + 1) * blk, :]
            if own and u == cb - 1:
                keep = causal
            else:
                keep = sel_ref[ln, pl.ds(sel0 + first_blk + u, 1), :] > 0.0
            parts.append(jnp.where(keep, su, -jnp.inf))
        return parts

    def chunk_pv(h, first_blk, p_parts):
        acc = None
        for u in range(cb):
            term = _dot(vt_ref[h, first_blk + pad + u], p_parts[u].astype(BF16))
            acc = term if acc is None else acc + term
        return acc

    qpp = MOBA_QBLOCKS_PER_STEP
    lanes = [(h, e) for e in range(qpp) for h in range(heads)]
    ls = range(len(lanes))

    def qblocks(qp, carry):
        qbs = [qp * qpp + e for _, e in lanes]
        rows = [pl.multiple_of(qb * blk, blk) for qb in qbs]
        qs = [_norm_rope(q_ref[pl.ds(rows[i], blk), h * d:(h + 1) * d], qgain_ref[...],
                         cos_ref[pl.ds(rows[i], blk), :], sin_ref[pl.ds(rows[i], blk), :])
              for i, (h, _) in enumerate(lanes)]
        q16s = [(q * scale).astype(BF16) for q in qs]
        gates = [lax.dot_general(kmean_ref[h], qs[i], (((1,), (1,)), ((), ())),
                                 precision=lax.Precision.HIGHEST, preferred_element_type=F32)
                 for i, (h, _) in enumerate(lanes)]
        for i in ls:
            rank = jnp.zeros((nblk, blk), jnp.int32)
            for m in range(nblk):
                gm = gates[i][m:m + 1, :]
                beats = (gm > gates[i]) | ((gm == gates[i]) & (m < nidx))
                rank = rank + jnp.where(beats & (m < qbs[i]), 1, 0)
            sel_ref[i, sel0:sel0 + nblk, :] = jnp.where((nidx < qbs[i]) & (rank < MOBA_TOP_K), 1.0, 0.0)
        parts0 = [chunk_scores(h, i, q16s[i], qbs[i] - pad, own=True) for i, (h, _) in enumerate(lanes)]
        m0 = [functools.reduce(jnp.maximum, [jnp.max(s, axis=0, keepdims=True) for s in parts0[i]]) for i in ls]
        p0 = [[jnp.exp2(s - m0[i]) for s in parts0[i]] for i in ls]
        l0 = [functools.reduce(jnp.add, [jnp.sum(p, axis=0, keepdims=True) for p in p0[i]]) for i in ls]
        stats = [(m0[i], l0[i], chunk_pv(h, qbs[i] - pad, p0[i])) for i, (h, _) in enumerate(lanes)]

        def older(t, st):
            firsts = [qbs[i] - pad - cb * t for i in ls]
            parts = [chunk_scores(h, i, q16s[i], firsts[i], own=False) for i, (h, _) in enumerate(lanes)]
            m_new = [functools.reduce(jnp.maximum, [st[i][0]] + [jnp.max(s, axis=0, keepdims=True) for s in parts[i]])
                     for i in ls]
            alpha = [jnp.exp2(st[i][0] - m_new[i]) for i in ls]
            p_parts = [[jnp.exp2(s - m_new[i]) for s in parts[i]] for i in ls]
            l_new = [alpha[i] * st[i][1] + functools.reduce(jnp.add, [jnp.sum(p, axis=0, keepdims=True) for p in p_parts[i]])
                     for i in ls]
            acc = [alpha[i] * st[i][2] + chunk_pv(h, firsts[i], p_parts[i]) for i, (h, _) in enumerate(lanes)]
            return tuple((m_new[i], l_new[i], acc[i]) for i in ls)

        final = lax.fori_loop(1, (qp * qpp) // cb + 1, older, tuple(stats))
        for i, (h, _) in enumerate(lanes):
            _, l_f, acc_f = final[i]
            o_ref[pl.ds(rows[i], blk), h * d:(h + 1) * d] = (acc_f / l_f).T.astype(o_ref.dtype)
        return carry

    lax.fori_loop(0, nblk // qpp, qblocks, 0)


def moba_core(proj, q_off, k_off, v_off, q_gain, k_gain, cos_t, sin_t, batch, seq):
    d = HEAD_DIM
    hp = MOBA_HEADS_PER_STEP
    m = proj.shape[0]
    nblk = seq // MOBA_BLOCK
    pad = MOBA_CHUNK - 1
    assert MOBA_CHUNK % MOBA_QBLOCKS_PER_STEP == 0 and nblk % MOBA_QBLOCKS_PER_STEP == 0
    qb, kb, vb = q_off // (hp * d), k_off // (hp * d), v_off // (hp * d)
    return pl.pallas_call(
        functools.partial(_moba_kernel, nblk=nblk),
        grid=(batch, MOBA_HEADS // hp),
        in_specs=[pl.BlockSpec((seq, hp * d), lambda b, h: (b, qb + h)),
                  pl.BlockSpec((seq, hp * d), lambda b, h: (b, kb + h)),
                  pl.BlockSpec((seq, hp * d), lambda b, h: (b, vb + h)),
                  pl.BlockSpec((1, d), lambda b, h: (0, 0)),
                  pl.BlockSpec((1, d), lambda b, h: (0, 0)),
                  pl.BlockSpec((seq, d), lambda b, h: (0, 0)),
                  pl.BlockSpec((seq, d), lambda b, h: (0, 0))],
        out_specs=pl.BlockSpec((seq, hp * d), lambda b, h: (b, h)),
        out_shape=jax.ShapeDtypeStruct((m, MOBA_HEADS * d), BF16),
        scratch_shapes=[pltpu.VMEM((hp, (nblk + pad) * MOBA_BLOCK, d), BF16),
                        pltpu.VMEM((hp, nblk + pad, d, MOBA_BLOCK), BF16),
                        pltpu.VMEM((hp, nblk, d), F32),
                        pltpu.VMEM((hp * MOBA_QBLOCKS_PER_STEP, SUBLANES + nblk, MOBA_BLOCK), F32)],
        compiler_params=_params("parallel", "parallel"),
        name="moba_core",
    )(proj, proj, proj, q_gain.reshape(1, d), k_gain.reshape(1, d), cos_t, sin_t)


def _swiglu_half_step(x, norm, w_gate, w_up, w_down):
    h = rmsnorm_bf16(x, norm)
    act = ffn_up(h, w_gate, w_up)
    return matmul_residual(act, w_down.astype(BF16), x, 0.5, tm=512, tn=512)


def _rope_tables(seq):
    half = HEAD_DIM // 2
    inv_freq = ROPE_THETA ** (-jnp.arange(half, dtype=F32) / half)
    ang = jnp.arange(seq).astype(F32)[:, None] * inv_freq[None, :]
    cos, sin = jnp.cos(ang), jnp.sin(ang)
    return jnp.concatenate([cos, cos], axis=1), jnp.concatenate([-sin, sin], axis=1)


def _layer(x, ffn1_norm, ffn1_w_gate, ffn1_w_up, ffn1_w_down, mix_norm, w_in, gdn_conv_w, gdn_A_log,
           gdn_dt_bias, gdn_o_norm, moba_q_norm, moba_k_norm, w_branch_gdn, w_branch_moba, w_out,
           ffn2_norm, ffn2_w_gate, ffn2_w_up, ffn2_w_down, batch, seq):
    d = HEAD_DIM
    qkw = GDN_QK_HEADS * d
    vw = GDN_V_HEADS * d
    mw = MOBA_HEADS * d
    dm = x.shape[1]

    x = _swiglu_half_step(x, ffn1_norm, ffn1_w_gate, ffn1_w_up, ffn1_w_down)

    ab_off = 2 * qkw + 2 * vw
    rest_off = ab_off + 2 * GDN_V_HEADS
    w_ab = jnp.pad(w_in[:, ab_off:rest_off], ((0, 0), (0, d - 2 * GDN_V_HEADS))).astype(BF16)
    w_in_t = w_in.T
    mq_off, mk_off, mv_off, gg_off, gm_off = 0, mw, 2 * mw, 3 * mw, 3 * mw + dm

    h = rmsnorm_bf16(x, mix_norm)
    proj_g = matmul_nt(h, w_in_t, 0, ab_off, F32)
    ab = matmul(h, w_ab, d, F32)
    proj_m = matmul_nt(h, w_in_t, rest_off, w_in.shape[1] - rest_off, F32)

    rep = GDN_V_HEADS // GDN_QK_HEADS
    to_heads = lambda t: t.reshape(batch, seq, GDN_QK_HEADS, rep).transpose(0, 2, 3, 1)
    a_t = to_heads(ab[:, :GDN_V_HEADS])
    b_t = to_heads(ab[:, GDN_V_HEADS:2 * GDN_V_HEADS])
    per_head = lambda t: jnp.broadcast_to(t.astype(F32).reshape(GDN_QK_HEADS, rep, 1), (GDN_QK_HEADS, rep, GDN_GROUP))
    o_gdn = gdn_core(proj_g, gdn_conv_w, a_t, b_t, per_head(gdn_A_log), per_head(gdn_dt_bias), gdn_o_norm, batch, seq)

    cos_t, sin_t = _rope_tables(seq)
    o_moba = moba_core(proj_m, mq_off, mk_off, mv_off, moba_q_norm, moba_k_norm, cos_t, sin_t, batch, seq)

    y = merge_branches(o_gdn, o_moba, w_branch_gdn.astype(BF16), w_branch_moba.astype(BF16), proj_m, gg_off, gm_off)
    x = matmul_residual(y, w_out, x, 1.0, tm=1024, tn=512)

    return _swiglu_half_step(x, ffn2_norm, ffn2_w_gate, ffn2_w_up, ffn2_w_down)


def kernel(x, ffn1_norm, ffn1_w_gate, ffn1_w_up, ffn1_w_down, mix_norm, w_in, gdn_conv_w, gdn_A_log, gdn_dt_bias,
           gdn_o_norm, moba_q_norm, moba_k_norm, w_branch_gdn, w_branch_moba, w_out, ffn2_norm, ffn2_w_gate,
           ffn2_w_up, ffn2_w_down):
    batch, seq, dm = x.shape
    assert ffn1_norm.shape[0] == 1, "single-layer stack"
    out = _layer(x.reshape(batch * seq, dm), ffn1_norm[0], ffn1_w_gate[0], ffn1_w_up[0], ffn1_w_down[0],
                 mix_norm[0], w_in[0], gdn_conv_w[0], gdn_A_log[0], gdn_dt_bias[0], gdn_o_norm[0],
                 moba_q_norm[0], moba_k_norm[0], w_branch_gdn[0], w_branch_moba[0], w_out[0],
                 ffn2_norm[0], ffn2_w_gate[0], ffn2_w_up[0], ffn2_w_down[0], batch, seq)
    return out.reshape(batch, seq, dm)
```

```python
import functools

import jax
import jax.numpy as jnp
from jax import lax
from jax.experimental import pallas as pl
from jax.experimental.pallas import tpu as pltpu

F32 = jnp.float32
BF16 = jnp.bfloat16

NORM_EPS = 1e-6
HEAD_DIM = 128
GDN_QK_HEADS = 16
GDN_V_HEADS = 32
GDN_CONV = 4
GDN_GROUP = 256
GDN_GROUPS_PER_STEP = 4
MOBA_HEADS = 16
MOBA_HEADS_PER_STEP = 2
MOBA_BLOCK = 256
MOBA_TOP_K = 3
MOBA_CHUNK = 4
MOBA_QBLOCKS_PER_STEP = 2
ROPE_THETA = 10000.0
LOG2_E = 1.4426950408889634
CONV_HALO = 8
SUBLANES = 8

VMEM_LIMIT = 56 * 1024 * 1024


def _params(*sem):
    return pltpu.CompilerParams(dimension_semantics=sem, vmem_limit_bytes=VMEM_LIMIT)


def _sigmoid(x):
    return 1.0 / (1.0 + jnp.exp(-x))


def _dot(a, b):
    return jnp.dot(a, b, preferred_element_type=F32)


def _dot_nt(a, b):
    return lax.dot_general(a, b, (((1,), (1,)), ((), ())), preferred_element_type=F32)


def _dot_tn(a, b):
    return lax.dot_general(a, b, (((0,), (0,)), ((), ())), preferred_element_type=F32)


def _rmsnorm_kernel(x_ref, g_ref, o_ref):
    x = x_ref[...]
    ms = jnp.mean(x * x, axis=-1, keepdims=True)
    o_ref[...] = (x * lax.rsqrt(ms + NORM_EPS) * g_ref[...]).astype(o_ref.dtype)


def rmsnorm_bf16(x, gain, tm=512):
    m, d = x.shape
    return pl.pallas_call(
        _rmsnorm_kernel,
        grid=(m // tm,),
        in_specs=[pl.BlockSpec((tm, d), lambda i: (i, 0)),
                  pl.BlockSpec((1, d), lambda i: (0, 0))],
        out_specs=pl.BlockSpec((tm, d), lambda i: (i, 0)),
        out_shape=jax.ShapeDtypeStruct((m, d), BF16),
        compiler_params=_params("parallel"),
        name="rmsnorm",
    )(x, gain.reshape(1, d))


def _ffn_up_kernel(h_ref, wg_ref, wu_ref, o_ref):
    h = h_ref[...]
    g = _dot(h, wg_ref[...].astype(BF16))
    u = _dot(h, wu_ref[...].astype(BF16))
    o_ref[...] = (g * _sigmoid(g) * u).astype(o_ref.dtype)


def ffn_up(h, wg, wu, tm=1024, tn=256):
    m, k = h.shape
    n = wg.shape[1]
    return pl.pallas_call(
        _ffn_up_kernel,
        grid=(m // tm, n // tn),
        in_specs=[pl.BlockSpec((tm, k), lambda i, j: (i, 0)),
                  pl.BlockSpec((k, tn), lambda i, j: (0, j)),
                  pl.BlockSpec((k, tn), lambda i, j: (0, j))],
        out_specs=pl.BlockSpec((tm, tn), lambda i, j: (i, j)),
        out_shape=jax.ShapeDtypeStruct((m, n), BF16),
        compiler_params=_params("parallel", "arbitrary"),
        name="ffn_up",
    )(h, wg, wu)


def _mm_kernel(a_ref, b_ref, o_ref):
    o_ref[...] = _dot(a_ref[...], b_ref[...].astype(BF16)).astype(o_ref.dtype)


def matmul(a, b, n, out_dtype, tm=1024, tn=512):
    m, k = a.shape
    tn = min(tn, n)
    return pl.pallas_call(
        _mm_kernel,
        grid=(m // tm, n // tn),
        in_specs=[pl.BlockSpec((tm, k), lambda i, j: (i, 0)),
                  pl.BlockSpec((k, tn), lambda i, j: (0, j))],
        out_specs=pl.BlockSpec((tm, tn), lambda i, j: (i, j)),
        out_shape=jax.ShapeDtypeStruct((m, n), out_dtype),
        compiler_params=_params("parallel", "arbitrary"),
        name="matmul",
    )(a, b)


def _mm_nt_kernel(a_ref, bt_ref, o_ref):
    o_ref[...] = _dot_nt(a_ref[...], bt_ref[...].astype(BF16)).astype(o_ref.dtype)


def matmul_nt(a, bt, row0, n, out_dtype, tm=1024, tn=512):
    m, k = a.shape
    assert row0 % SUBLANES == 0 and n % tn == 0 and row0 + n <= bt.shape[0]
    return pl.pallas_call(
        _mm_nt_kernel,
        grid=(m // tm, n // tn),
        in_specs=[pl.BlockSpec((tm, k), lambda i, j: (i, 0)),
                  pl.BlockSpec((pl.Element(tn), pl.Element(k)),
                               lambda i, j: (pl.multiple_of(row0 + j * tn, SUBLANES), 0))],
        out_specs=pl.BlockSpec((tm, tn), lambda i, j: (i, j)),
        out_shape=jax.ShapeDtypeStruct((m, n), out_dtype),
        compiler_params=_params("parallel", "arbitrary"),
        name="matmul_nt",
    )(a, bt)


def _mm_residual_kernel(a_ref, b_ref, r_ref, o_ref, *, scale):
    o_ref[...] = r_ref[...] + scale * _dot(a_ref[...], b_ref[...].astype(BF16))


def matmul_residual(a, b, res, scale, tm, tn):
    m, k = a.shape
    n = b.shape[1]
    return pl.pallas_call(
        functools.partial(_mm_residual_kernel, scale=scale),
        grid=(m // tm, n // tn),
        in_specs=[pl.BlockSpec((tm, k), lambda i, j: (i, 0)),
                  pl.BlockSpec((k, tn), lambda i, j: (0, j)),
                  pl.BlockSpec((tm, tn), lambda i, j: (i, j))],
        out_specs=pl.BlockSpec((tm, tn), lambda i, j: (i, j)),
        out_shape=jax.ShapeDtypeStruct((m, n), F32),
        compiler_params=_params("parallel", "arbitrary"),
        name="matmul_residual",
    )(a, b, res)


def _merge_kernel(og_ref, om_ref, wg_ref, wm_ref, gg_ref, gm_ref, o_ref):
    yg = _dot(og_ref[...], wg_ref[...])
    ym = _dot(om_ref[...], wm_ref[...])
    o_ref[...] = (_sigmoid(gg_ref[...]) * yg + _sigmoid(gm_ref[...]) * ym).astype(o_ref.dtype)


def merge_branches(o_gdn, o_moba, wg, wm, proj, gg_off, gm_off, tm=1024, tn=512):
    m, kg = o_gdn.shape
    km = o_moba.shape[1]
    n = wg.shape[1]
    ggb, gmb = gg_off // tn, gm_off // tn
    return pl.pallas_call(
        _merge_kernel,
        grid=(m // tm, n // tn),
        in_specs=[pl.BlockSpec((tm, kg), lambda i, j: (i, 0)),
                  pl.BlockSpec((tm, km), lambda i, j: (i, 0)),
                  pl.BlockSpec((kg, tn), lambda i, j: (0, j)),
                  pl.BlockSpec((km, tn), lambda i, j: (0, j)),
                  pl.BlockSpec((tm, tn), lambda i, j: (i, ggb + j)),
                  pl.BlockSpec((tm, tn), lambda i, j: (i, gmb + j))],
        out_specs=pl.BlockSpec((tm, tn), lambda i, j: (i, j)),
        out_shape=jax.ShapeDtypeStruct((m, n), BF16),
        compiler_params=_params("parallel", "arbitrary"),
        name="merge_branches",
    )(o_gdn, o_moba, wg, wm, proj, proj)


def _split_bf16(a):
    hi = a.astype(BF16)
    lo = (a - hi.astype(F32)).astype(BF16)
    return hi, lo


def _mm3(a_split, b_split):
    ah, al = a_split
    bh, bl = b_split
    lhs = jnp.concatenate([ah, ah, al], axis=1)
    rhs = jnp.concatenate([bh, bl, bh], axis=0)
    return _dot(lhs, rhs)


def _odd_blocks(x, n):
    return jnp.concatenate([x[(2 * i + 1) * n:(2 * i + 2) * n] for i in range(x.shape[0] // (2 * n))], axis=0)


def _zero_even_blocks(y, n):
    zeros = jnp.zeros((n, y.shape[1]), y.dtype)
    pieces = []
    for i in range(y.shape[0] // n):
        pieces += [zeros, y[i * n:(i + 1) * n]]
    return jnp.concatenate(pieces, axis=0)


def _sub_from_odd_blocks(x, y, n):
    pieces = []
    for i in range(x.shape[0] // (2 * n)):
        pieces += [x[2 * i * n:(2 * i + 1) * n], x[(2 * i + 1) * n:(2 * i + 2) * n] - y[i * n:(i + 1) * n]]
    return jnp.concatenate(pieces, axis=0)


def _unit_lower_inverses(lmats, eye, ij_xor):
    size = eye.shape[0]
    dinvs = [eye - jnp.where(ij_xor == 1, lm, 0.0) for lm in lmats]
    level = 1
    while (2 << level) <= size:
        n = 1 << level
        mask = (ij_xor >> level) == 1
        d16 = [dv.astype(BF16) for dv in dinvs]
        if n < SUBLANES:
            cd = [_dot(jnp.where(mask, lm, 0.0).astype(BF16), dh) for lm, dh in zip(lmats, d16)]
            dinvs = [dv - _dot(dh, m.astype(BF16)) for dv, dh, m in zip(dinvs, d16, cd)]
        else:
            mask_odd = (_odd_blocks(ij_xor, n) >> level) == 1
            cd = [_dot(jnp.where(mask_odd, _odd_blocks(lm, n), 0.0).astype(BF16), dh) for lm, dh in zip(lmats, d16)]
            upd = [_dot(_odd_blocks(dv, n).astype(BF16), _zero_even_blocks(m, n).astype(BF16))
                   for dv, m in zip(dinvs, cd)]
            dinvs = [_sub_from_odd_blocks(dv, u, n) for dv, u in zip(dinvs, upd)]
        level += 1
    resid = [eye - dv - _mm3(_split_bf16(lm), _split_bf16(dv)) for lm, dv in zip(lmats, dinvs)]
    return [dv + _dot(dv.astype(BF16), r.astype(BF16)) for dv, r in zip(dinvs, resid)]


def _conv_silu(x_ref, w_ref, r0, rows):
    x = x_ref[pl.ds(r0, rows), :]
    prev = x_ref[pl.ds(pl.multiple_of(jnp.maximum(r0 - CONV_HALO, 0), CONV_HALO), CONV_HALO), :]
    xx = jnp.concatenate([jnp.where(r0 > 0, prev, 0.0), x], axis=0)
    acc = x * w_ref[GDN_CONV - 1:GDN_CONV, :]
    for j in range(GDN_CONV - 1):
        acc = acc + pltpu.roll(xx, GDN_CONV - 1 - j, axis=0)[CONV_HALO:CONV_HALO + rows] * w_ref[j:j + 1, :]
    return acc * _sigmoid(acc)


def _l2_normalize(x):
    return x * lax.rsqrt(jnp.sum(x * x, axis=-1, keepdims=True) + NORM_EPS)


def _gdn_kernel(q_ref, k_ref, v_ref, z_ref, wq_ref, wk_ref, wv_ref, a_ref, b_ref, alog_ref, dtb_ref, onorm_ref,
                o_ref, *, seq):
    c = GDN_GROUP
    d = HEAD_DIM
    heads = v_ref.shape[1] // d
    ngrp = GDN_GROUPS_PER_STEP
    ii = lax.broadcasted_iota(jnp.int32, (c, c), 0)
    jj = lax.broadcasted_iota(jnp.int32, (c, c), 1)
    lower = ii >= jj
    strict = ii > jj
    eye_b = ii == jj
    eye = eye_b.astype(F32)
    ij_xor = ii ^ jj
    scale = d ** -0.5

    def step(it, states):
        rows, qs, ks, vs, kks, qks = [], [], [], [], [], []
        for gg in range(ngrp):
            r0 = pl.multiple_of((it * ngrp + gg) * c, c)
            q = _l2_normalize(_conv_silu(q_ref, wq_ref, r0, c)) * scale
            k = _l2_normalize(_conv_silu(k_ref, wk_ref, r0, c))
            k16 = k.astype(BF16)
            rows.append(r0)
            qs.append(q)
            ks.append(k)
            vs.append(_conv_silu(v_ref, wv_ref, r0, c))
            kks.append(_dot_nt(k16, k16))
            qks.append(_dot_nt(q.astype(BF16), k16))
        prob = []
        for gg in range(ngrp):
            for h in range(heads):
                a_row = a_ref[h:h + 1, pl.ds(rows[gg], c)]
                b_row = b_ref[h:h + 1, pl.ds(rows[gg], c)]
                x = a_row + dtb_ref[h:h + 1, :]
                softplus = jnp.maximum(x, 0.0) + jnp.log1p(jnp.exp(-jnp.abs(x)))
                g_row = -jnp.exp(alog_ref[h:h + 1, :]) * softplus
                beta_row = _sigmoid(b_row)
                gc_col = jnp.sum(jnp.where(lower, g_row, 0.0), axis=1, keepdims=True)
                gc_row = jnp.sum(jnp.where(eye_b, gc_col, 0.0), axis=0, keepdims=True)
                beta_col = jnp.sum(jnp.where(eye_b, beta_row, 0.0), axis=1, keepdims=True)
                decay = jnp.where(lower, jnp.exp(jnp.where(lower, gc_col - gc_row, 0.0)), 0.0)
                lmat = jnp.where(strict, beta_col * kks[gg] * decay, 0.0)
                prob.append(dict(gg=gg, h=h, gc_col=gc_col, beta_col=beta_col, decay=decay, lmat=lmat))
        tinv = _unit_lower_inverses([p["lmat"] for p in prob], eye, ij_xor)
        for p, t in zip(prob, tinv):
            gg, h, gc_col = p["gg"], p["h"], p["gc_col"]
            v = vs[gg][:, h * d:(h + 1) * d]
            rhs = p["beta_col"] * jnp.concatenate([v, ks[gg] * jnp.exp(gc_col)], axis=1)
            uw = _dot(t.astype(BF16), rhs.astype(BF16))
            p["u"] = uw[:, :d]
            p["w16"] = uw[:, d:].astype(BF16)
            p["attn16"] = jnp.where(lower, qks[gg] * p["decay"], 0.0).astype(BF16)
            p["qg16"] = (qs[gg] * jnp.exp(gc_col)).astype(BF16)
            g_last = gc_col[c - 1:c, :]
            p["kdec16"] = (ks[gg] * jnp.exp(g_last - gc_col)).astype(BF16)
            p["carry"] = jnp.exp(g_last)
        states = list(states)
        for p in prob:
            gg, h = p["gg"], p["h"]
            s = states[h]
            s16 = s.astype(BF16)
            v_new = p["u"] - _dot(p["w16"], s16)
            vn16 = v_new.astype(BF16)
            o = _dot(p["qg16"], s16) + _dot(p["attn16"], vn16)
            states[h] = s * p["carry"] + _dot_tn(p["kdec16"], vn16)
            ms = jnp.mean(o * o, axis=-1, keepdims=True)
            z = z_ref[pl.ds(rows[gg], c), h * d:(h + 1) * d]
            y = o * lax.rsqrt(ms + NORM_EPS) * onorm_ref[...] * (z * _sigmoid(z))
            o_ref[pl.ds(rows[gg], c), h * d:(h + 1) * d] = y.astype(o_ref.dtype)
        return tuple(states)

    init = tuple(jnp.zeros((d, d), F32) for _ in range(heads))
    lax.fori_loop(0, seq // (c * ngrp), step, init)


def gdn_core(proj, conv_w, a_t, b_t, alog, dtb, o_norm, batch, seq):
    d = HEAD_DIM
    rep = GDN_V_HEADS // GDN_QK_HEADS
    m = proj.shape[0]
    kb = GDN_QK_HEADS
    vb = 2 * GDN_QK_HEADS // rep
    zb = vb + GDN_V_HEADS // rep
    assert seq % (GDN_GROUP * GDN_GROUPS_PER_STEP) == 0
    return pl.pallas_call(
        functools.partial(_gdn_kernel, seq=seq),
        grid=(batch, GDN_QK_HEADS),
        in_specs=[pl.BlockSpec((seq, d), lambda b, j: (b, j)),
                  pl.BlockSpec((seq, d), lambda b, j: (b, kb + j)),
                  pl.BlockSpec((seq, rep * d), lambda b, j: (b, vb + j)),
                  pl.BlockSpec((seq, rep * d), lambda b, j: (b, zb + j)),
                  pl.BlockSpec((GDN_CONV, d), lambda b, j: (0, j)),
                  pl.BlockSpec((GDN_CONV, d), lambda b, j: (0, kb + j)),
                  pl.BlockSpec((GDN_CONV, rep * d), lambda b, j: (0, vb + j)),
                  pl.BlockSpec((None, None, rep, seq), lambda b, j: (b, j, 0, 0)),
                  pl.BlockSpec((None, None, rep, seq), lambda b, j: (b, j, 0, 0)),
                  pl.BlockSpec((None, rep, GDN_GROUP), lambda b, j: (j, 0, 0)),
                  pl.BlockSpec((None, rep, GDN_GROUP), lambda b, j: (j, 0, 0)),
                  pl.BlockSpec((1, d), lambda b, j: (0, 0))],
        out_specs=pl.BlockSpec((seq, rep * d), lambda b, j: (b, j)),
        out_shape=jax.ShapeDtypeStruct((m, GDN_V_HEADS * d), BF16),
        compiler_params=_params("parallel", "parallel"),
        name="gdn_core",
    )(proj, proj, proj, proj, conv_w, conv_w, conv_w, a_t, b_t, alog, dtb, o_norm.reshape(1, d))


def _norm_rope(x, gain, cos, sin):
    ms = jnp.mean(x * x, axis=-1, keepdims=True)
    y = x * lax.rsqrt(ms + NORM_EPS) * gain
    return y * cos + pltpu.roll(y, HEAD_DIM // 2, axis=1) * sin


def _moba_kernel(q_ref, k_ref, v_ref, qgain_ref, kgain_ref, cos_ref, sin_ref, o_ref,
                 kpad_ref, vt_ref, kmean_ref, q16_ref, sel_ref, *, nblk):
    blk = MOBA_BLOCK
    d = HEAD_DIM
    cb = MOBA_CHUNK
    pad = cb - 1
    heads = q_ref.shape[1] // d
    scale = d ** -0.5 * LOG2_E
    sel0 = SUBLANES
    nidx = lax.broadcasted_iota(jnp.int32, (nblk, blk), 0)

    for h in range(heads):
        kpad_ref[h, 0:pad * blk, :] = jnp.zeros((pad * blk, d), BF16)
        vt_ref[h, 0:pad] = jnp.zeros((pad, d, blk), BF16)
        for n in range(nblk):
            kb = _norm_rope(k_ref[n * blk:(n + 1) * blk, h * d:(h + 1) * d], kgain_ref[...],
                            cos_ref[n * blk:(n + 1) * blk, :], sin_ref[n * blk:(n + 1) * blk, :])
            kmean_ref[h, n:n + 1, :] = jnp.mean(kb, axis=0, keepdims=True)
            kpad_ref[h, (pad + n) * blk:(pad + n + 1) * blk, :] = kb.astype(BF16)
            vt_ref[h, pad + n] = v_ref[n * blk:(n + 1) * blk, h * d:(h + 1) * d].T.astype(BF16)
        for n in range(nblk):
            q = _norm_rope(q_ref[n * blk:(n + 1) * blk, h * d:(h + 1) * d], qgain_ref[...],
                           cos_ref[n * blk:(n + 1) * blk, :], sin_ref[n * blk:(n + 1) * blk, :])
            q16_ref[h, n] = (q * scale).astype(BF16)
            gate = lax.dot_general(kmean_ref[h], q, (((1,), (1,)), ((), ())),
                                   precision=lax.Precision.HIGHEST, preferred_element_type=F32)
            rank = jnp.zeros((nblk, blk), jnp.int32)
            for m in range(n):
                gm = gate[m:m + 1, :]
                rank = rank + jnp.where((gm > gate) | ((gm == gate) & (m < nidx)), 1, 0)
            sel_ref[h, n, 0:sel0, :] = jnp.zeros((sel0, blk), F32)
            sel_ref[h, n, sel0:sel0 + nblk, :] = jnp.where((nidx < n) & (rank < MOBA_TOP_K), 1.0, 0.0)

    kpos = lax.broadcasted_iota(jnp.int32, (blk, blk), 0)
    qpos = lax.broadcasted_iota(jnp.int32, (blk, blk), 1)
    causal = kpos <= qpos

    def chunk_scores(h, qb, first_blk, own):
        k0 = pl.multiple_of((first_blk + pad) * blk, blk)
        st = _dot_nt(kpad_ref[h, pl.ds(k0, cb * blk), :], q16_ref[h, qb])
        parts = []
        for u in range(cb):
            su = st[u * blk:(u + 1) * blk, :]
            if own and u == cb - 1:
                keep = causal
            else:
                keep = sel_ref[h, qb, pl.ds(sel0 + first_blk + u, 1), :] > 0.0
            parts.append(jnp.where(keep, su, -jnp.inf))
        return parts

    def chunk_pv(h, first_blk, p_parts):
        acc = None
        for u in range(cb):
            term = _dot(vt_ref[h, first_blk + pad + u], p_parts[u].astype(BF16))
            acc = term if acc is None else acc + term
        return acc

    qpp = MOBA_QBLOCKS_PER_STEP
    lanes = [(h, e) for e in range(qpp) for h in range(heads)]
    ls = range(len(lanes))

    def qblocks(qp, carry):
        qbs = [qp * qpp + e for _, e in lanes]
        rows = [pl.multiple_of(qb * blk, blk) for qb in qbs]
        parts0 = [chunk_scores(h, qbs[i], qbs[i] - pad, own=True) for i, (h, _) in enumerate(lanes)]
        m0 = [functools.reduce(jnp.maximum, [jnp.max(s, axis=0, keepdims=True) for s in parts0[i]]) for i in ls]
        p0 = [[jnp.exp2(s - m0[i]) for s in parts0[i]] for i in ls]
        l0 = [functools.reduce(jnp.add, [jnp.sum(p, axis=0, keepdims=True) for p in p0[i]]) for i in ls]
        stats = [(m0[i], l0[i], chunk_pv(h, qbs[i] - pad, p0[i])) for i, (h, _) in enumerate(lanes)]

        def older(t, st):
            firsts = [qbs[i] - pad - cb * t for i in ls]
            parts = [chunk_scores(h, qbs[i], firsts[i], own=False) for i, (h, _) in enumerate(lanes)]
            m_new = [functools.reduce(jnp.maximum, [st[i][0]] + [jnp.max(s, axis=0, keepdims=True) for s in parts[i]])
                     for i in ls]
            alpha = [jnp.exp2(st[i][0] - m_new[i]) for i in ls]
            p_parts = [[jnp.exp2(s - m_new[i]) for s in parts[i]] for i in ls]
            l_new = [alpha[i] * st[i][1] + functools.reduce(jnp.add, [jnp.sum(p, axis=0, keepdims=True) for p in p_parts[i]])
                     for i in ls]
            acc = [alpha[i] * st[i][2] + chunk_pv(h, firsts[i], p_parts[i]) for i, (h, _) in enumerate(lanes)]
            return tuple((m_new[i], l_new[i], acc[i]) for i in ls)

        final = lax.fori_loop(1, (qp * qpp) // cb + 1, older, tuple(stats))
        for i, (h, _) in enumerate(lanes):
            _, l_f, acc_f = final[i]
            o_ref[pl.ds(rows[i], blk), h * d:(h + 1) * d] = (acc_f / l_f).T.astype(o_ref.dtype)
        return carry

    lax.fori_loop(0, nblk // qpp, qblocks, 0)


def moba_core(proj, q_off, k_off, v_off, q_gain, k_gain, cos_t, sin_t, batch, seq):
    d = HEAD_DIM
    hp = MOBA_HEADS_PER_STEP
    m = proj.shape[0]
    nblk = seq // MOBA_BLOCK
    pad = MOBA_CHUNK - 1
    assert MOBA_CHUNK % MOBA_QBLOCKS_PER_STEP == 0 and nblk % MOBA_QBLOCKS_PER_STEP == 0
    qb, kb, vb = q_off // (hp * d), k_off // (hp * d), v_off // (hp * d)
    return pl.pallas_call(
        functools.partial(_moba_kernel, nblk=nblk),
        grid=(batch, MOBA_HEADS // hp),
        in_specs=[pl.BlockSpec((seq, hp * d), lambda b, h: (b, qb + h)),
                  pl.BlockSpec((seq, hp * d), lambda b, h: (b, kb + h)),
                  pl.BlockSpec((seq, hp * d), lambda b, h: (b, vb + h)),
                  pl.BlockSpec((1, d), lambda b, h: (0, 0)),
                  pl.BlockSpec((1, d), lambda b, h: (0, 0)),
                  pl.BlockSpec((seq, d), lambda b, h: (0, 0)),
                  pl.BlockSpec((seq, d), lambda b, h: (0, 0))],
        out_specs=pl.BlockSpec((seq, hp * d), lambda b, h: (b, h)),
        out_shape=jax.ShapeDtypeStruct((m, MOBA_HEADS * d), BF16),
        scratch_shapes=[pltpu.VMEM((hp, (nblk + pad) * MOBA_BLOCK, d), BF16),
                        pltpu.VMEM((hp, nblk + pad, d, MOBA_BLOCK), BF16),
                        pltpu.VMEM((hp, nblk, d), F32),
                        pltpu.VMEM((hp, nblk, MOBA_BLOCK, d), BF16),
                        pltpu.VMEM((hp, nblk, SUBLANES + nblk, MOBA_BLOCK), F32)],
        compiler_params=_params("parallel", "parallel"),
        name="moba_core",
    )(proj, proj, proj, q_gain.reshape(1, d), k_gain.reshape(1, d), cos_t, sin_t)


def _swiglu_half_step(x, norm, w_gate, w_up, w_down):
    h = rmsnorm_bf16(x, norm)
    act = ffn_up(h, w_gate, w_up)
    return matmul_residual(act, w_down.astype(BF16), x, 0.5, tm=512, tn=512)


def _rope_tables(seq):
    half = HEAD_DIM // 2
    inv_freq = ROPE_THETA ** (-jnp.arange(half, dtype=F32) / half)
    ang = jnp.arange(seq).astype(F32)[:, None] * inv_freq[None, :]
    cos, sin = jnp.cos(ang), jnp.sin(ang)
    return jnp.concatenate([cos, cos], axis=1), jnp.concatenate([-sin, sin], axis=1)


def _layer(x, ffn1_norm, ffn1_w_gate, ffn1_w_up, ffn1_w_down, mix_norm, w_in, gdn_conv_w, gdn_A_log,
           gdn_dt_bias, gdn_o_norm, moba_q_norm, moba_k_norm, w_branch_gdn, w_branch_moba, w_out,
           ffn2_norm, ffn2_w_gate, ffn2_w_up, ffn2_w_down, batch, seq):
    d = HEAD_DIM
    qkw = GDN_QK_HEADS * d
    vw = GDN_V_HEADS * d
    mw = MOBA_HEADS * d
    dm = x.shape[1]

    x = _swiglu_half_step(x, ffn1_norm, ffn1_w_gate, ffn1_w_up, ffn1_w_down)

    ab_off = 2 * qkw + 2 * vw
    rest_off = ab_off + 2 * GDN_V_HEADS
    w_ab = jnp.pad(w_in[:, ab_off:rest_off], ((0, 0), (0, d - 2 * GDN_V_HEADS))).astype(BF16)
    w_in_t = w_in.T
    mq_off, mk_off, mv_off, gg_off, gm_off = 0, mw, 2 * mw, 3 * mw, 3 * mw + dm

    h = rmsnorm_bf16(x, mix_norm)
    proj_g = matmul_nt(h, w_in_t, 0, ab_off, F32)
    ab = matmul(h, w_ab, d, F32)
    proj_m = matmul_nt(h, w_in_t, rest_off, w_in.shape[1] - rest_off, F32)

    rep = GDN_V_HEADS // GDN_QK_HEADS
    to_heads = lambda t: t.reshape(batch, seq, GDN_QK_HEADS, rep).transpose(0, 2, 3, 1)
    a_t = to_heads(ab[:, :GDN_V_HEADS])
    b_t = to_heads(ab[:, GDN_V_HEADS:2 * GDN_V_HEADS])
    per_head = lambda t: jnp.broadcast_to(t.astype(F32).reshape(GDN_QK_HEADS, rep, 1), (GDN_QK_HEADS, rep, GDN_GROUP))
    o_gdn = gdn_core(proj_g, gdn_conv_w, a_t, b_t, per_head(gdn_A_log), per_head(gdn_dt_bias), gdn_o_norm, batch, seq)

    cos_t, sin_t = _rope_tables(seq)
    o_moba = moba_core(proj_m, mq_off, mk_off, mv_off, moba_q_norm, moba_k_norm, cos_t, sin_t, batch, seq)

    y = merge_branches(o_gdn, o_moba, w_branch_gdn.astype(BF16), w_branch_moba.astype(BF16), proj_m, gg_off, gm_off)
    x = matmul_residual(y, w_out, x, 1.0, tm=1024, tn=512)

    return _swiglu_half_step(x, ffn2_norm, ffn2_w_gate, ffn2_w_up, ffn2_w_down)


def kernel(x, ffn1_norm, ffn1_w_gate, ffn1_w_up, ffn1_w_down, mix_norm, w_in, gdn_conv_w, gdn_A_log, gdn_dt_bias,
           gdn_o_norm, moba_q_norm, moba_k_norm, w_branch_gdn, w_branch_moba, w_out, ffn2_norm, ffn2_w_gate,
           ffn2_w_up, ffn2_w_down):
    batch, seq, dm = x.shape
    assert ffn1_norm.shape[0] == 1, "single-layer stack"
    out = _layer(x.reshape(batch * seq, dm), ffn1_norm[0], ffn1_w_gate[0], ffn1_w_up[0], ffn1_w_down[0],
                 mix_norm[0], w_in[0], gdn_conv_w[0], gdn_A_log[0], gdn_dt_bias[0], gdn_o_norm[0],
                 moba_q_norm[0], moba_k_norm[0], w_branch_gdn[0], w_branch_moba[0], w_out[0],
                 ffn2_norm[0], ffn2_w_gate[0], ffn2_w_up[0], ffn2_w_down[0], batch, seq)
    return out.reshape(batch, seq, dm)
```

```python
import functools

import jax
import jax.numpy as jnp
from jax import lax
from jax.experimental import pallas as pl
from jax.experimental.pallas import tpu as pltpu

F32 = jnp.float32
BF16 = jnp.bfloat16

NORM_EPS = 1e-6
HEAD_DIM = 128
GDN_QK_HEADS = 16
GDN_V_HEADS = 32
GDN_CONV = 4
GDN_GROUP = 256
GDN_GROUPS_PER_STEP = 4
MOBA_HEADS = 16
MOBA_HEADS_PER_STEP = 2
MOBA_BLOCK = 256
MOBA_TOP_K = 3
MOBA_CHUNK = 4
MOBA_QBLOCKS_PER_STEP = 2
ROPE_THETA = 10000.0
LOG2_E = 1.4426950408889634
CONV_HALO = 8
SUBLANES = 8

VMEM_LIMIT = 56 * 1024 * 1024


def _params(*sem):
    return pltpu.CompilerParams(dimension_semantics=sem, vmem_limit_bytes=VMEM_LIMIT)


def _sigmoid(x):
    return 1.0 / (1.0 + jnp.exp(-x))


def _dot(a, b):
    return jnp.dot(a, b, preferred_element_type=F32)


def _dot_nt(a, b):
    return lax.dot_general(a, b, (((1,), (1,)), ((), ())), preferred_element_type=F32)


def _dot_tn(a, b):
    return lax.dot_general(a, b, (((0,), (0,)), ((), ())), preferred_element_type=F32)


def _rmsnorm_kernel(x_ref, g_ref, o_ref):
    x = x_ref[...]
    ms = jnp.mean(x * x, axis=-1, keepdims=True)
    o_ref[...] = (x * lax.rsqrt(ms + NORM_EPS) * g_ref[...]).astype(o_ref.dtype)


def rmsnorm_bf16(x, gain, tm=512):
    m, d = x.shape
    return pl.pallas_call(
        _rmsnorm_kernel,
        grid=(m // tm,),
        in_specs=[pl.BlockSpec((tm, d), lambda i: (i, 0)),
                  pl.BlockSpec((1, d), lambda i: (0, 0))],
        out_specs=pl.BlockSpec((tm, d), lambda i: (i, 0)),
        out_shape=jax.ShapeDtypeStruct((m, d), BF16),
        compiler_params=_params("parallel"),
        name="rmsnorm",
    )(x, gain.reshape(1, d))


def _ffn_up_kernel(h_ref, wg_ref, wu_ref, o_ref):
    h = h_ref[...]
    g = _dot(h, wg_ref[...].astype(BF16))
    u = _dot(h, wu_ref[...].astype(BF16))
    o_ref[...] = (g * _sigmoid(g) * u).astype(o_ref.dtype)


def ffn_up(h, wg, wu, tm=1024, tn=256):
    m, k = h.shape
    n = wg.shape[1]
    return pl.pallas_call(
        _ffn_up_kernel,
        grid=(m // tm, n // tn),
        in_specs=[pl.BlockSpec((tm, k), lambda i, j: (i, 0)),
                  pl.BlockSpec((k, tn), lambda i, j: (0, j)),
                  pl.BlockSpec((k, tn), lambda i, j: (0, j))],
        out_specs=pl.BlockSpec((tm, tn), lambda i, j: (i, j)),
        out_shape=jax.ShapeDtypeStruct((m, n), BF16),
        compiler_params=_params("parallel", "arbitrary"),
        name="ffn_up",
    )(h, wg, wu)


def _mm_kernel(a_ref, b_ref, o_ref):
    o_ref[...] = _dot(a_ref[...], b_ref[...].astype(BF16)).astype(o_ref.dtype)


def matmul(a, b, n, out_dtype, tm=1024, tn=512):
    m, k = a.shape
    tn = min(tn, n)
    return pl.pallas_call(
        _mm_kernel,
        grid=(m // tm, n // tn),
        in_specs=[pl.BlockSpec((tm, k), lambda i, j: (i, 0)),
                  pl.BlockSpec((k, tn), lambda i, j: (0, j))],
        out_specs=pl.BlockSpec((tm, tn), lambda i, j: (i, j)),
        out_shape=jax.ShapeDtypeStruct((m, n), out_dtype),
        compiler_params=_params("parallel", "arbitrary"),
        name="matmul",
    )(a, b)


def _mm_nt_kernel(a_ref, bt_ref, o_ref):
    o_ref[...] = _dot_nt(a_ref[...], bt_ref[...].astype(BF16)).astype(o_ref.dtype)


def matmul_nt(a, bt, row0, n, out_dtype, tm=1024, tn=512):
    m, k = a.shape
    assert row0 % SUBLANES == 0 and n % tn == 0 and row0 + n <= bt.shape[0]
    return pl.pallas_call(
        _mm_nt_kernel,
        grid=(m // tm, n // tn),
        in_specs=[pl.BlockSpec((tm, k), lambda i, j: (i, 0)),
                  pl.BlockSpec((pl.Element(tn), pl.Element(k)),
                               lambda i, j: (pl.multiple_of(row0 + j * tn, SUBLANES), 0))],
        out_specs=pl.BlockSpec((tm, tn), lambda i, j: (i, j)),
        out_shape=jax.ShapeDtypeStruct((m, n), out_dtype),
        compiler_params=_params("parallel", "arbitrary"),
        name="matmul_nt",
    )(a, bt)


def _mm_residual_kernel(a_ref, b_ref, r_ref, o_ref, *, scale):
    o_ref[...] = r_ref[...] + scale * _dot(a_ref[...], b_ref[...].astype(BF16))


def matmul_residual(a, b, res, scale, tm, tn):
    m, k = a.shape
    n = b.shape[1]
    return pl.pallas_call(
        functools.partial(_mm_residual_kernel, scale=scale),
        grid=(m // tm, n // tn),
        in_specs=[pl.BlockSpec((tm, k), lambda i, j: (i, 0)),
                  pl.BlockSpec((k, tn), lambda i, j: (0, j)),
                  pl.BlockSpec((tm, tn), lambda i, j: (i, j))],
        out_specs=pl.BlockSpec((tm, tn), lambda i, j: (i, j)),
        out_shape=jax.ShapeDtypeStruct((m, n), F32),
        compiler_params=_params("parallel", "arbitrary"),
        name="matmul_residual",
    )(a, b, res)


def _merge_kernel(og_ref, om_ref, wg_ref, wm_ref, gg_ref, gm_ref, o_ref):
    yg = _dot(og_ref[...], wg_ref[...])
    ym = _dot(om_ref[...], wm_ref[...])
    o_ref[...] = (_sigmoid(gg_ref[...]) * yg + _sigmoid(gm_ref[...]) * ym).astype(o_ref.dtype)


def merge_branches(o_gdn, o_moba, wg, wm, proj, gg_off, gm_off, tm=1024, tn=512):
    m, kg = o_gdn.shape
    km = o_moba.shape[1]
    n = wg.shape[1]
    ggb, gmb = gg_off // tn, gm_off // tn
    return pl.pallas_call(
        _merge_kernel,
        grid=(m // tm, n // tn),
        in_specs=[pl.BlockSpec((tm, kg), lambda i, j: (i, 0)),
                  pl.BlockSpec((tm, km), lambda i, j: (i, 0)),
                  pl.BlockSpec((kg, tn), lambda i, j: (0, j)),
                  pl.BlockSpec((km, tn), lambda i, j: (0, j)),
                  pl.BlockSpec((tm, tn), lambda i, j: (i, ggb + j)),
                  pl.BlockSpec((tm, tn), lambda i, j: (i, gmb + j))],
        out_specs=pl.BlockSpec((tm, tn), lambda i, j: (i, j)),
        out_shape=jax.ShapeDtypeStruct((m, n), BF16),
        compiler_params=_params("parallel", "arbitrary"),
        name="merge_branches",
    )(o_gdn, o_moba, wg, wm, proj, proj)


def _split_bf16(a):
    hi = a.astype(BF16)
    lo = (a - hi.astype(F32)).astype(BF16)
    return hi, lo


def _mm3(a_split, b_split):
    ah, al = a_split
    bh, bl = b_split
    lhs = jnp.concatenate([ah, ah, al], axis=1)
    rhs = jnp.concatenate([bh, bl, bh], axis=0)
    return _dot(lhs, rhs)


def _odd_blocks(x, n):
    return jnp.concatenate([x[(2 * i + 1) * n:(2 * i + 2) * n] for i in range(x.shape[0] // (2 * n))], axis=0)


def _zero_even_blocks(y, n):
    zeros = jnp.zeros((n, y.shape[1]), y.dtype)
    pieces = []
    for i in range(y.shape[0] // n):
        pieces += [zeros, y[i * n:(i + 1) * n]]
    return jnp.concatenate(pieces, axis=0)


def _sub_from_odd_blocks(x, y, n):
    pieces = []
    for i in range(x.shape[0] // (2 * n)):
        pieces += [x[2 * i * n:(2 * i + 1) * n], x[(2 * i + 1) * n:(2 * i + 2) * n] - y[i * n:(i + 1) * n]]
    return jnp.concatenate(pieces, axis=0)


def _unit_lower_inverses(lmats, eye, ij_xor):
    size = eye.shape[0]
    dinvs = [eye - jnp.where(ij_xor == 1, lm, 0.0) for lm in lmats]
    level = 1
    while (2 << level) <= size:
        n = 1 << level
        mask = (ij_xor >> level) == 1
        d16 = [dv.astype(BF16) for dv in dinvs]
        if n < SUBLANES:
            cd = [_dot(jnp.where(mask, lm, 0.0).astype(BF16), dh) for lm, dh in zip(lmats, d16)]
            dinvs = [dv - _dot(dh, m.astype(BF16)) for dv, dh, m in zip(dinvs, d16, cd)]
        else:
            mask_odd = (_odd_blocks(ij_xor, n) >> level) == 1
            cd = [_dot(jnp.where(mask_odd, _odd_blocks(lm, n), 0.0).astype(BF16), dh) for lm, dh in zip(lmats, d16)]
            upd = [_dot(_odd_blocks(dv, n).astype(BF16), _zero_even_blocks(m, n).astype(BF16))
                   for dv, m in zip(dinvs, cd)]
            dinvs = [_sub_from_odd_blocks(dv, u, n) for dv, u in zip(dinvs, upd)]
        level += 1
    resid = [eye - dv - _mm3(_split_bf16(lm), _split_bf16(dv)) for lm, dv in zip(lmats, dinvs)]
    return [dv + _dot(dv.astype(BF16), r.astype(BF16)) for dv, r in zip(dinvs, resid)]


def _conv_silu(x_ref, w_ref, r0, rows):
    x = x_ref[pl.ds(r0, rows), :]
    prev = x_ref[pl.ds(pl.multiple_of(jnp.maximum(r0 - CONV_HALO, 0), CONV_HALO), CONV_HALO), :]
    xx = jnp.concatenate([jnp.where(r0 > 0, prev, 0.0), x], axis=0)
    acc = x * w_ref[GDN_CONV - 1:GDN_CONV, :]
    for j in range(GDN_CONV - 1):
        acc = acc + pltpu.roll(xx, GDN_CONV - 1 - j, axis=0)[CONV_HALO:CONV_HALO + rows] * w_ref[j:j + 1, :]
    return acc * _sigmoid(acc)


def _l2_normalize(x):
    return x * lax.rsqrt(jnp.sum(x * x, axis=-1, keepdims=True) + NORM_EPS)


def _gdn_kernel(q_ref, k_ref, v_ref, z_ref, wq_ref, wk_ref, wv_ref, a_ref, b_ref, alog_ref, dtb_ref, onorm_ref,
                o_ref, *, seq):
    c = GDN_GROUP
    d = HEAD_DIM
    heads = v_ref.shape[1] // d
    ngrp = GDN_GROUPS_PER_STEP
    ii = lax.broadcasted_iota(jnp.int32, (c, c), 0)
    jj = lax.broadcasted_iota(jnp.int32, (c, c), 1)
    lower = ii >= jj
    strict = ii > jj
    eye_b = ii == jj
    eye = eye_b.astype(F32)
    ij_xor = ii ^ jj
    scale = d ** -0.5

    def step(it, states):
        rows, qs, ks, vs, kks, qks = [], [], [], [], [], []
        for gg in range(ngrp):
            r0 = pl.multiple_of((it * ngrp + gg) * c, c)
            q = _l2_normalize(_conv_silu(q_ref, wq_ref, r0, c)) * scale
            k = _l2_normalize(_conv_silu(k_ref, wk_ref, r0, c))
            k16 = k.astype(BF16)
            rows.append(r0)
            qs.append(q)
            ks.append(k)
            vs.append(_conv_silu(v_ref, wv_ref, r0, c))
            kks.append(_dot_nt(k16, k16))
            qks.append(_dot_nt(q.astype(BF16), k16))
        prob = []
        for gg in range(ngrp):
            for h in range(heads):
                a_row = a_ref[h:h + 1, pl.ds(rows[gg], c)]
                b_row = b_ref[h:h + 1, pl.ds(rows[gg], c)]
                x = a_row + dtb_ref[h:h + 1, :]
                softplus = jnp.maximum(x, 0.0) + jnp.log1p(jnp.exp(-jnp.abs(x)))
                g_row = -jnp.exp(alog_ref[h:h + 1, :]) * softplus
                beta_row = _sigmoid(b_row)
                gc_col = jnp.sum(jnp.where(lower, g_row, 0.0), axis=1, keepdims=True)
                gc_row = jnp.sum(jnp.where(eye_b, gc_col, 0.0), axis=0, keepdims=True)
                beta_col = jnp.sum(jnp.where(eye_b, beta_row, 0.0), axis=1, keepdims=True)
                decay = jnp.where(lower, jnp.exp(jnp.where(lower, gc_col - gc_row, 0.0)), 0.0)
                lmat = jnp.where(strict, beta_col * kks[gg] * decay, 0.0)
                prob.append(dict(gg=gg, h=h, gc_col=gc_col, beta_col=beta_col, decay=decay, lmat=lmat))
        tinv = _unit_lower_inverses([p["lmat"] for p in prob], eye, ij_xor)
        for p, t in zip(prob, tinv):
            gg, h, gc_col = p["gg"], p["h"], p["gc_col"]
            v = vs[gg][:, h * d:(h + 1) * d]
            rhs = p["beta_col"] * jnp.concatenate([v, ks[gg] * jnp.exp(gc_col)], axis=1)
            uw = _dot(t.astype(BF16), rhs.astype(BF16))
            p["u"] = uw[:, :d]
            p["w16"] = uw[:, d:].astype(BF16)
            p["attn16"] = jnp.where(lower, qks[gg] * p["decay"], 0.0).astype(BF16)
            p["qg16"] = (qs[gg] * jnp.exp(gc_col)).astype(BF16)
            g_last = gc_col[c - 1:c, :]
            p["kdec16"] = (ks[gg] * jnp.exp(g_last - gc_col)).astype(BF16)
            p["carry"] = jnp.exp(g_last)
        states = list(states)
        for p in prob:
            gg, h = p["gg"], p["h"]
            s = states[h]
            s16 = s.astype(BF16)
            v_new = p["u"] - _dot(p["w16"], s16)
            vn16 = v_new.astype(BF16)
            o = _dot(p["qg16"], s16) + _dot(p["attn16"], vn16)
            states[h] = s * p["carry"] + _dot_tn(p["kdec16"], vn16)
            ms = jnp.mean(o * o, axis=-1, keepdims=True)
            z = z_ref[pl.ds(rows[gg], c), h * d:(h + 1) * d]
            y = o * lax.rsqrt(ms + NORM_EPS) * onorm_ref[...] * (z * _sigmoid(z))
            o_ref[pl.ds(rows[gg], c), h * d:(h + 1) * d] = y.astype(o_ref.dtype)
        return tuple(states)

    init = tuple(jnp.zeros((d, d), F32) for _ in range(heads))
    lax.fori_loop(0, seq // (c * ngrp), step, init)


def gdn_core(proj, conv_w, a_t, b_t, alog, dtb, o_norm, batch, seq):
    d = HEAD_DIM
    rep = GDN_V_HEADS // GDN_QK_HEADS
    m = proj.shape[0]
    kb = GDN_QK_HEADS
    vb = 2 * GDN_QK_HEADS // rep
    zb = vb + GDN_V_HEADS // rep
    assert seq % (GDN_GROUP * GDN_GROUPS_PER_STEP) == 0
    return pl.pallas_call(
        functools.partial(_gdn_kernel, seq=seq),
        grid=(batch, GDN_QK_HEADS),
        in_specs=[pl.BlockSpec((seq, d), lambda b, j: (b, j)),
                  pl.BlockSpec((seq, d), lambda b, j: (b, kb + j)),
                  pl.BlockSpec((seq, rep * d), lambda b, j: (b, vb + j)),
                  pl.BlockSpec((seq, rep * d), lambda b, j: (b, zb + j)),
                  pl.BlockSpec((GDN_CONV, d), lambda b, j: (0, j)),
                  pl.BlockSpec((GDN_CONV, d), lambda b, j: (0, kb + j)),
                  pl.BlockSpec((GDN_CONV, rep * d), lambda b, j: (0, vb + j)),
                  pl.BlockSpec((None, None, rep, seq), lambda b, j: (b, j, 0, 0)),
                  pl.BlockSpec((None, None, rep, seq), lambda b, j: (b, j, 0, 0)),
                  pl.BlockSpec((None, rep, GDN_GROUP), lambda b, j: (j, 0, 0)),
                  pl.BlockSpec((None, rep, GDN_GROUP), lambda b, j: (j, 0, 0)),
                  pl.BlockSpec((1, d), lambda b, j: (0, 0))],
        out_specs=pl.BlockSpec((seq, rep * d), lambda b, j: (b, j)),
        out_shape=jax.ShapeDtypeStruct((m, GDN_V_HEADS * d), BF16),
        compiler_params=_params("parallel", "parallel"),
        name="gdn_core",
    )(proj, proj, proj, proj, conv_w, conv_w, conv_w, a_t, b_t, alog, dtb, o_norm.reshape(1, d))


def _norm_rope(x, gain, cos, sin):
    ms = jnp.mean(x * x, axis=-1, keepdims=True)
    y = x * lax.rsqrt(ms + NORM_EPS) * gain
    return y * cos + pltpu.roll(y, HEAD_DIM // 2, axis=1) * sin


def _moba_kernel(q_ref, k_ref, v_ref, qgain_ref, kgain_ref, cos_ref, sin_ref, o_ref,
                 kpad_ref, vt_ref, kmean_ref, q16_ref, sel_ref, *, nblk):
    blk = MOBA_BLOCK
    d = HEAD_DIM
    cb = MOBA_CHUNK
    pad = cb - 1
    heads = q_ref.shape[1] // d
    scale = d ** -0.5 * LOG2_E
    sel0 = SUBLANES
    nidx = lax.broadcasted_iota(jnp.int32, (nblk, blk), 0)
    eye16 = (lax.broadcasted_iota(jnp.int32, (d, d), 0) == lax.broadcasted_iota(jnp.int32, (d, d), 1)).astype(BF16)

    for h in range(heads):
        kpad_ref[h, 0:pad * blk, :] = jnp.zeros((pad * blk, d), BF16)
        vt_ref[h, 0:pad] = jnp.zeros((pad, d, blk), BF16)
        for n in range(nblk):
            kb = _norm_rope(k_ref[n * blk:(n + 1) * blk, h * d:(h + 1) * d], kgain_ref[...],
                            cos_ref[n * blk:(n + 1) * blk, :], sin_ref[n * blk:(n + 1) * blk, :])
            kmean_ref[h, n:n + 1, :] = jnp.mean(kb, axis=0, keepdims=True)
            kpad_ref[h, (pad + n) * blk:(pad + n + 1) * blk, :] = kb.astype(BF16)
            v16 = v_ref[n * blk:(n + 1) * blk, h * d:(h + 1) * d].astype(BF16)
            vt_ref[h, pad + n] = _dot_nt(eye16, v16).astype(BF16)
        for n in range(nblk):
            q = _norm_rope(q_ref[n * blk:(n + 1) * blk, h * d:(h + 1) * d], qgain_ref[...],
                           cos_ref[n * blk:(n + 1) * blk, :], sin_ref[n * blk:(n + 1) * blk, :])
            q16_ref[h, n] = (q * scale).astype(BF16)
            gate = lax.dot_general(kmean_ref[h], q, (((1,), (1,)), ((), ())),
                                   precision=lax.Precision.HIGHEST, preferred_element_type=F32)
            rank = jnp.zeros((nblk, blk), jnp.int32)
            for m in range(n):
                gm = gate[m:m + 1, :]
                rank = rank + jnp.where((gm > gate) | ((gm == gate) & (m < nidx)), 1, 0)
            sel_ref[h, n, 0:sel0, :] = jnp.zeros((sel0, blk), F32)
            sel_ref[h, n, sel0:sel0 + nblk, :] = jnp.where((nidx < n) & (rank < MOBA_TOP_K), 1.0, 0.0)

    kpos = lax.broadcasted_iota(jnp.int32, (blk, blk), 0)
    qpos = lax.broadcasted_iota(jnp.int32, (blk, blk), 1)
    causal = kpos <= qpos

    def chunk_scores(h, qb, first_blk, own):
        q16 = q16_ref[h, qb]
        parts = []
        for u in range(cb):
            k0 = pl.multiple_of((first_blk + pad + u) * blk, blk)
            su = _dot_nt(kpad_ref[h, pl.ds(k0, blk), :], q16)
            if own and u == cb - 1:
                keep = causal
            else:
                keep = sel_ref[h, qb, pl.ds(sel0 + first_blk + u, 1), :] > 0.0
            parts.append(jnp.where(keep, su, -jnp.inf))
        return parts

    def chunk_pv(h, first_blk, p_parts):
        acc = None
        for u in range(cb):
            term = _dot(vt_ref[h, first_blk + pad + u], p_parts[u].astype(BF16))
            acc = term if acc is None else acc + term
        return acc

    qpp = MOBA_QBLOCKS_PER_STEP
    lanes = [(h, e) for e in range(qpp) for h in range(heads)]
    ls = range(len(lanes))

    def qblocks(qp, carry):
        qbs = [qp * qpp + e for _, e in lanes]
        rows = [pl.multiple_of(qb * blk, blk) for qb in qbs]
        parts0 = [chunk_scores(h, qbs[i], qbs[i] - pad, own=True) for i, (h, _) in enumerate(lanes)]
        m0 = [functools.reduce(jnp.maximum, [jnp.max(s, axis=0, keepdims=True) for s in parts0[i]]) for i in ls]
        p0 = [[jnp.exp2(s - m0[i]) for s in parts0[i]] for i in ls]
        l0 = [functools.reduce(jnp.add, [jnp.sum(p, axis=0, keepdims=True) for p in p0[i]]) for i in ls]
        stats = [(m0[i], l0[i], chunk_pv(h, qbs[i] - pad, p0[i])) for i, (h, _) in enumerate(lanes)]

        def older(t, st):
            firsts = [qbs[i] - pad - cb * t for i in ls]
            parts = [chunk_scores(h, qbs[i], firsts[i], own=False) for i, (h, _) in enumerate(lanes)]
            m_new = [functools.reduce(jnp.maximum, [st[i][0]] + [jnp.max(s, axis=0, keepdims=True) for s in parts[i]])
                     for i in ls]
            alpha = [jnp.exp2(st[i][0] - m_new[i]) for i in ls]
            p_parts = [[jnp.exp2(s - m_new[i]) for s in parts[i]] for i in ls]
            l_new = [alpha[i] * st[i][1] + functools.reduce(jnp.add, [jnp.sum(p, axis=0, keepdims=True) for p in p_parts[i]])
                     for i in ls]
            acc = [alpha[i] * st[i][2] + chunk_pv(h, firsts[i], p_parts[i]) for i, (h, _) in enumerate(lanes)]
            return tuple((m_new[i], l_new[i], acc[i]) for i in ls)

        final = lax.fori_loop(1, (qp * qpp) // cb + 1, older, tuple(stats))
        for i, (h, _) in enumerate(lanes):
            _, l_f, acc_f = final[i]
            o_ref[pl.ds(rows[i], blk), h * d:(h + 1) * d] = (acc_f / l_f).T.astype(o_ref.dtype)
        return carry

    lax.fori_loop(0, nblk // qpp, qblocks, 0)


def moba_core(proj, q_off, k_off, v_off, q_gain, k_gain, cos_t, sin_t, batch, seq):
    d = HEAD_DIM
    hp = MOBA_HEADS_PER_STEP
    m = proj.shape[0]
    nblk = seq // MOBA_BLOCK
    pad = MOBA_CHUNK - 1
    assert MOBA_CHUNK % MOBA_QBLOCKS_PER_STEP == 0 and nblk % MOBA_QBLOCKS_PER_STEP == 0
    qb, kb, vb = q_off // (hp * d), k_off // (hp * d), v_off // (hp * d)
    return pl.pallas_call(
        functools.partial(_moba_kernel, nblk=nblk),
        grid=(batch, MOBA_HEADS // hp),
        in_specs=[pl.BlockSpec((seq, hp * d), lambda b, h: (b, qb + h)),
                  pl.BlockSpec((seq, hp * d), lambda b, h: (b, kb + h)),
                  pl.BlockSpec((seq, hp * d), lambda b, h: (b, vb + h)),
                  pl.BlockSpec((1, d), lambda b, h: (0, 0)),
                  pl.BlockSpec((1, d), lambda b, h: (0, 0)),
                  pl.BlockSpec((seq, d), lambda b, h: (0, 0)),
                  pl.BlockSpec((seq, d), lambda b, h: (0, 0))],
        out_specs=pl.BlockSpec((seq, hp * d), lambda b, h: (b, h)),
        out_shape=jax.ShapeDtypeStruct((m, MOBA_HEADS * d), BF16),
        scratch_shapes=[pltpu.VMEM((hp, (nblk + pad) * MOBA_BLOCK, d), BF16),
                        pltpu.VMEM((hp, nblk + pad, d, MOBA_BLOCK), BF16),
                        pltpu.VMEM((hp, nblk, d), F32),
                        pltpu.VMEM((hp, nblk, MOBA_BLOCK, d), BF16),
                        pltpu.VMEM((hp, nblk, SUBLANES + nblk, MOBA_BLOCK), F32)],
        compiler_params=_params("parallel", "parallel"),
        name="moba_core",
    )(proj, proj, proj, q_gain.reshape(1, d), k_gain.reshape(1, d), cos_t, sin_t)


def _swiglu_half_step(x, norm, w_gate, w_up, w_down):
    h = rmsnorm_bf16(x, norm)
    act = ffn_up(h, w_gate, w_up)
    return matmul_residual(act, w_down.astype(BF16), x, 0.5, tm=512, tn=512)


def _rope_tables(seq):
    half = HEAD_DIM // 2
    inv_freq = ROPE_THETA ** (-jnp.arange(half, dtype=F32) / half)
    ang = jnp.arange(seq).astype(F32)[:, None] * inv_freq[None, :]
    cos, sin = jnp.cos(ang), jnp.sin(ang)
    return jnp.concatenate([cos, cos], axis=1), jnp.concatenate([-sin, sin], axis=1)


def _layer(x, ffn1_norm, ffn1_w_gate, ffn1_w_up, ffn1_w_down, mix_norm, w_in, gdn_conv_w, gdn_A_log,
           gdn_dt_bias, gdn_o_norm, moba_q_norm, moba_k_norm, w_branch_gdn, w_branch_moba, w_out,
           ffn2_norm, ffn2_w_gate, ffn2_w_up, ffn2_w_down, batch, seq):
    d = HEAD_DIM
    qkw = GDN_QK_HEADS * d
    vw = GDN_V_HEADS * d
    mw = MOBA_HEADS * d
    dm = x.shape[1]

    x = _swiglu_half_step(x, ffn1_norm, ffn1_w_gate, ffn1_w_up, ffn1_w_down)

    ab_off = 2 * qkw + 2 * vw
    rest_off = ab_off + 2 * GDN_V_HEADS
    w_ab = jnp.pad(w_in[:, ab_off:rest_off], ((0, 0), (0, d - 2 * GDN_V_HEADS))).astype(BF16)
    w_in_t = w_in.T
    mq_off, mk_off, mv_off, gg_off, gm_off = 0, mw, 2 * mw, 3 * mw, 3 * mw + dm

    h = rmsnorm_bf16(x, mix_norm)
    proj_g = matmul_nt(h, w_in_t, 0, ab_off, F32)
    ab = matmul(h, w_ab, d, F32)
    proj_m = matmul_nt(h, w_in_t, rest_off, w_in.shape[1] - rest_off, F32)

    rep = GDN_V_HEADS // GDN_QK_HEADS
    to_heads = lambda t: t.reshape(batch, seq, GDN_QK_HEADS, rep).transpose(0, 2, 3, 1)
    a_t = to_heads(ab[:, :GDN_V_HEADS])
    b_t = to_heads(ab[:, GDN_V_HEADS:2 * GDN_V_HEADS])
    per_head = lambda t: jnp.broadcast_to(t.astype(F32).reshape(GDN_QK_HEADS, rep, 1), (GDN_QK_HEADS, rep, GDN_GROUP))
    o_gdn = gdn_core(proj_g, gdn_conv_w, a_t, b_t, per_head(gdn_A_log), per_head(gdn_dt_bias), gdn_o_norm, batch, seq)

    cos_t, sin_t = _rope_tables(seq)
    o_moba = moba_core(proj_m, mq_off, mk_off, mv_off, moba_q_norm, moba_k_norm, cos_t, sin_t, batch, seq)

    y = merge_branches(o_gdn, o_moba, w_branch_gdn.astype(BF16), w_branch_moba.astype(BF16), proj_m, gg_off, gm_off)
    x = matmul_residual(y, w_out.astype(BF16), x, 1.0, tm=1024, tn=512)

    return _swiglu_half_step(x, ffn2_norm, ffn2_w_gate, ffn2_w_up, ffn2_w_down)


def kernel(x, ffn1_norm, ffn1_w_gate, ffn1_w_up, ffn1_w_down, mix_norm, w_in, gdn_conv_w, gdn_A_log, gdn_dt_bias,
           gdn_o_norm, moba_q_norm, moba_k_norm, w_branch_gdn, w_branch_moba, w_out, ffn2_norm, ffn2_w_gate,
           ffn2_w_up, ffn2_w_down):
    batch, seq, dm = x.shape
    assert ffn1_norm.shape[0] == 1, "single-layer stack"
    out = _layer(x.reshape(batch * seq, dm), ffn1_norm[0], ffn1_w_gate[0], ffn1_w_up[0], ffn1_w_down[0],
                 mix_norm[0], w_in[0], gdn_conv_w[0], gdn_A_log[0], gdn_dt_bias[0], gdn_o_norm[0],
                 moba_q_norm[0], moba_k_norm[0], w_branch_gdn[0], w_branch_moba[0], w_out[0],
                 ffn2_norm[0], ffn2_w_gate[0], ffn2_w_up[0], ffn2_w_down[0], batch, seq)
    return out.reshape(batch, seq, dm)
```
